```python
import jax
import jax.numpy as jnp
from jax import lax
import numpy as np

D_MODEL = 1024
BATCH = 4
SEQ = 4096
DEPTH = 4

GRID_W = 64
CTX_LEN = 256
N_EVEN = (DEPTH + 1) // 2
N_ODD = DEPTH // 2
A_Q_HEADS = 8
A_KV_HEADS = 2
HEAD_DIM = 64
WINDOW = 128
CONV_CH = 512
CONV_WIDTH = 31
MLA_HEADS = 16
MLA_NOPE = 64
MLA_ROPE = 32
MLA_V = 64
MLA_QK = MLA_NOPE + MLA_ROPE
Q_LORA = 768
KV_LORA = 256
Q_BLOCK = 128
N_EXPERTS = 32
TOP_K = 4
D_EXPERT = 1024
SWIGLU_LIMIT = 7.0
SWIGLU_ALPHA = 1.702
MOE_BLOCK = 256

ROPE_THETA = 10000.0
EPS = 1e-6
NEG_INF = -1e30
AB_IN = A_Q_HEADS * HEAD_DIM + 2 * A_KV_HEADS * HEAD_DIM + 2 * CONV_CH
AB_MIX = A_Q_HEADS * HEAD_DIM + CONV_CH
C_IN = Q_LORA + KV_LORA + MLA_ROPE

kernel_name = 'hybrid_dit_window_conv_mla_moe'


def rms_norm(x, g):
    xf = x.astype(jnp.float32)
    y = xf * lax.rsqrt(jnp.mean(xf * xf, axis=-1, keepdims=True) + EPS)
    return (y * g.astype(jnp.float32)).astype(x.dtype)


def layer_norm(x, g, b):
    xf = x.astype(jnp.float32)
    mu = jnp.mean(xf, axis=-1, keepdims=True)
    xc = xf - mu
    y = xc * lax.rsqrt(jnp.mean(xc * xc, axis=-1, keepdims=True) + EPS)
    return (y * g.astype(jnp.float32) + b.astype(jnp.float32)).astype(x.dtype)


def modulate(h, shift, scale):
    return h * (1.0 + scale) + shift


def axial_rope_tables(n_tok, rot_dims):
    rows = n_tok // GRID_W
    row = jnp.broadcast_to(jnp.arange(rows)[:, None], (rows, GRID_W)).reshape(-1).astype(jnp.float32)
    col = jnp.broadcast_to(jnp.arange(GRID_W)[None, :], (rows, GRID_W)).reshape(-1).astype(jnp.float32)
    axis_dims = rot_dims // 2
    inv = ROPE_THETA ** (-(jnp.arange(axis_dims // 2, dtype=jnp.float32) * 2.0) / axis_dims)
    ang = jnp.concatenate([row[:, None] * inv, col[:, None] * inv], axis=-1)
    return jnp.cos(ang), jnp.sin(ang)


def apply_axial_rope(x, cos, sin):
    n = x.shape[-1] // 4
    xf = x.astype(jnp.float32).reshape(x.shape[:-1] + (2, 2, n))
    c = cos.reshape(-1, 1, 2, n)
    s = sin.reshape(-1, 1, 2, n)
    x1 = xf[..., 0, :]
    x2 = xf[..., 1, :]
    out = jnp.stack([x1 * c - x2 * s, x2 * c + x1 * s], axis=-2)
    return out.reshape(x.shape).astype(x.dtype)


def window_sink_attention(q, k, v, k_ctx, v_ctx, sink):
    b, s, hq, hd = q.shape
    hkv = k.shape[2]
    g = hq // hkv
    w = WINDOW
    nb = s // w
    scale = hd ** -0.5
    qb = q.reshape(b, nb, w, hkv, g, hd)

    def band(t):
        tp = jnp.pad(t, ((0, 0), (w, w), (0, 0), (0, 0))).reshape(b, nb + 2, w, hkv, hd)
        return jnp.concatenate([tp[:, :-2], tp[:, 1:-1], tp[:, 2:]], axis=2)

    kb = band(k)
    vb = band(v)
    s_loc = jnp.einsum('bnqhgd,bnkhd->bnhgqk', qb, kb).astype(jnp.float32) * scale
    qi = jnp.arange(w)[:, None]
    ki = jnp.arange(3 * w)[None, :]
    rel = ki - w - qi
    kpos = jnp.arange(nb)[:, None, None] * w - w + ki[None]
    valid = (jnp.abs(rel) <= w)[None] & (kpos >= 0) & (kpos < s)
    s_loc = jnp.where(valid[None, :, None, None], s_loc, NEG_INF)
    s_ctx = jnp.einsum('bnqhgd,bkhd->bnhgqk', qb, k_ctx).astype(jnp.float32) * scale
    s_sink = jnp.broadcast_to(sink.astype(jnp.float32).reshape(1, 1, hkv, g, 1, 1), s_loc.shape[:-1] + (1,))
    p = jax.nn.softmax(jnp.concatenate([s_loc, s_ctx, s_sink], axis=-1), axis=-1)
    p_loc = p[..., :3 * w].astype(v.dtype)
    p_ctx = p[..., 3 * w:-1].astype(v.dtype)
    o = jnp.einsum('bnhgqk,bnkhd->bnqhgd', p_loc, vb) + jnp.einsum('bnhgqk,bkhd->bnqhgd', p_ctx, v_ctx)
    return o.reshape(b, s, hq * hd)


def ctx_sink_attention(q, k, v, sink):
    b, l, hq, hd = q.shape
    hkv = k.shape[2]
    g = hq // hkv
    qg = q.reshape(b, l, hkv, g, hd)
    sc = jnp.einsum('bqhgd,bkhd->bhgqk', qg, k).astype(jnp.float32) * (hd ** -0.5)
    sk = jnp.broadcast_to(sink.astype(jnp.float32).reshape(1, hkv, g, 1, 1), sc.shape[:-1] + (1,))
    p = jax.nn.softmax(jnp.concatenate([sc, sk], axis=-1), axis=-1)[..., :-1].astype(v.dtype)
    return jnp.einsum('bhgqk,bkhd->bqhgd', p, v).reshape(b, l, hq * hd)


def conformer_conv(u, dw_w, dw_b, ln_g, ln_b):
    a, gt = jnp.split(u, 2, axis=-1)
    z = a * jax.nn.sigmoid(gt)
    z = lax.conv_general_dilated(z, dw_w[:, None, :].astype(z.dtype), window_strides=(1,),
                                 padding=[(CONV_WIDTH // 2, CONV_WIDTH // 2)],
                                 dimension_numbers=('NWC', 'WIO', 'NWC'),
                                 feature_group_count=CONV_CH) + dw_b
    return jax.nn.silu(layer_norm(z, ln_g, ln_b))


def mixer_window_conv(h_lat, h_ctx, w_in, w_out, q_g, k_g, sink, dw_w, dw_b, ln_g, ln_b, cos, sin, with_ctx_out):
    dq = A_Q_HEADS * HEAD_DIM
    dkv = A_KV_HEADS * HEAD_DIM

    def project(h):
        b, l = h.shape[:2]
        p = h @ w_in
        q = rms_norm(p[..., :dq].reshape(b, l, A_Q_HEADS, HEAD_DIM), q_g)
        k = rms_norm(p[..., dq:dq + dkv].reshape(b, l, A_KV_HEADS, HEAD_DIM), k_g)
        v = p[..., dq + dkv:dq + 2 * dkv].reshape(b, l, A_KV_HEADS, HEAD_DIM)
        return q, k, v, p[..., dq + 2 * dkv:]

    q_l, k_l, v_l, u_l = project(h_lat)
    q_c, k_c, v_c, u_c = project(h_ctx)
    q_l = apply_axial_rope(q_l, cos, sin)
    k_l = apply_axial_rope(k_l, cos, sin)
    a_l = window_sink_attention(q_l, k_l, v_l, k_c, v_c, sink)
    b_l = conformer_conv(u_l, dw_w, dw_b, ln_g, ln_b)
    y_l = jnp.concatenate([a_l, b_l], axis=-1) @ w_out
    y_c = None
    if with_ctx_out:
        a_c = ctx_sink_attention(q_c, k_c, v_c, sink)
        b_c = conformer_conv(u_c, dw_w, dw_b, ln_g, ln_b)
        y_c = jnp.concatenate([a_c, b_c], axis=-1) @ w_out
    return y_l, y_c


def mla_attention(q, k_nope, k_rope, v):
    b, lq, h, dqk = q.shape
    scale = dqk ** -0.5
    nq = lq // Q_BLOCK
    qb = jnp.moveaxis(q.reshape(b, nq, Q_BLOCK, h, dqk), 1, 0)

    def one_block(qblk):
        sc = (jnp.einsum('bqhd,bkhd->bhqk', qblk[..., :MLA_NOPE], k_nope)
              + jnp.einsum('bqhr,bkr->bhqk', qblk[..., MLA_NOPE:], k_rope)).astype(jnp.float32) * scale
        p = jax.nn.softmax(sc, axis=-1).astype(v.dtype)
        return jnp.einsum('bhqk,bkhd->bqhd', p, v)

    o = lax.map(one_block, qb)
    return jnp.moveaxis(o, 0, 1).reshape(b, lq, h * MLA_V)


def mixer_mla(h_lat, h_ctx, w_in, q_a_g, kv_a_g, w_q_b, w_kv_b, q_g, k_g, kr_g, w_out, cos, sin, with_ctx_out):
    def project(h):
        b, l = h.shape[:2]
        p = h @ w_in
        cq = rms_norm(p[..., :Q_LORA], q_a_g)
        ckv = rms_norm(p[..., Q_LORA:Q_LORA + KV_LORA], kv_a_g)
        k_rope = rms_norm(p[..., Q_LORA + KV_LORA:], kr_g)
        q = rms_norm((cq @ w_q_b).reshape(b, l, MLA_HEADS, MLA_QK), q_g)
        kv = (ckv @ w_kv_b).reshape(b, l, MLA_HEADS, MLA_NOPE + MLA_V)
        k_nope = rms_norm(kv[..., :MLA_NOPE], k_g)
        return q, k_nope, k_rope, kv[..., MLA_NOPE:]

    q_l, kn_l, kr_l, v_l = project(h_lat)
    q_c, kn_c, kr_c, v_c = project(h_ctx)
    q_l = jnp.concatenate([q_l[..., :MLA_NOPE], apply_axial_rope(q_l[..., MLA_NOPE:], cos, sin)], axis=-1)
    kr_l = apply_axial_rope(kr_l[:, :, None, :], cos, sin)[:, :, 0, :]
    o_l = mla_attention(q_l, jnp.concatenate([kn_l, kn_c], axis=1), jnp.concatenate([kr_l, kr_c], axis=1),
                        jnp.concatenate([v_l, v_c], axis=1))
    y_l = o_l @ w_out
    y_c = None
    if with_ctx_out:
        y_c = mla_attention(q_c, kn_c, kr_c, v_c) @ w_out
    return y_l, y_c


def moe_ffn(h, router_w, router_b, w_gu, b_gu, w_down, b_down):
    t, d = h.shape
    logits = (h @ router_w + router_b).astype(jnp.float32)
    top_val, top_idx = lax.top_k(logits, TOP_K)
    gates = jax.nn.softmax(top_val, axis=-1)
    n = t * TOP_K
    e_flat = top_idx.reshape(-1)
    tok_flat = jnp.arange(n, dtype=jnp.int32) // TOP_K
    order = jnp.argsort(e_flat)
    e_sorted = e_flat[order]
    tok_sorted = tok_flat[order]
    gate_sorted = gates.reshape(-1)[order]
    counts = jnp.zeros((N_EXPERTS,), jnp.int32).at[e_flat].add(1)
    starts = jnp.cumsum(counts) - counts
    padded = (counts + MOE_BLOCK - 1) // MOE_BLOCK * MOE_BLOCK
    pad_ends = jnp.cumsum(padded)
    pad_starts = pad_ends - padded
    dest = pad_starts[e_sorted] + jnp.arange(n, dtype=jnp.int32) - starts[e_sorted]
    n_blocks = -(-n // MOE_BLOCK) + N_EXPERTS
    row_tok = jnp.zeros((n_blocks * MOE_BLOCK,), jnp.int32).at[dest].set(tok_sorted)
    block_start = jnp.arange(n_blocks, dtype=jnp.int32) * MOE_BLOCK
    block_expert = jnp.minimum(jnp.searchsorted(pad_ends, block_start, side='right'), N_EXPERTS - 1)
    xb = h[row_tok].reshape(n_blocks, MOE_BLOCK, d)

    def expert_block(args):
        xblk, e = args
        gu = xblk @ w_gu[e] + b_gu[e]
        gate = jnp.minimum(gu[:, :D_EXPERT], SWIGLU_LIMIT)
        up = jnp.clip(gu[:, D_EXPERT:], -SWIGLU_LIMIT, SWIGLU_LIMIT)
        act = (up + 1.0) * (gate * jax.nn.sigmoid(SWIGLU_ALPHA * gate))
        return act @ w_down[e] + b_down[e]

    yb = lax.map(expert_block, (xb, block_expert)).reshape(n_blocks * MOE_BLOCK, d)
    contrib = yb[dest] * gate_sorted[:, None].astype(yb.dtype)
    return jnp.zeros_like(h).at[tok_sorted].add(contrib)


def setup_inputs(seed: int = 0) -> dict:
    key = jax.random.key(seed)
    ks = jax.random.split(key, 40)
    f32 = jnp.float32
    D = D_MODEL

    def nrm(k, shape, scale):
        return jax.random.normal(k, shape, f32) * scale

    return {
        'x': nrm(ks[0], (BATCH, SEQ, D), 1.0),
        'c': nrm(ks[1], (BATCH, D), 1.0),
        'ctx': nrm(ks[2], (BATCH, CTX_LEN, D), 1.0),
        'c_ctx': nrm(ks[3], (D,), 1.0),
        'mod_w': nrm(ks[4], (DEPTH, D, 6 * D), 0.5 * D ** -0.5),
        'mod_b': nrm(ks[5], (DEPTH, 6 * D), 0.02),
        'norm_mix_g': 1.0 + nrm(ks[6], (DEPTH, D), 0.02),
        'norm_ffn_g': 1.0 + nrm(ks[7], (DEPTH, D), 0.02),
        'ab_w_in': nrm(ks[8], (N_EVEN, D, AB_IN), D ** -0.5),
        'ab_w_out': nrm(ks[9], (N_EVEN, AB_MIX, D), AB_MIX ** -0.5),
        'a_q_norm': 1.0 + nrm(ks[10], (N_EVEN, HEAD_DIM), 0.02),
        'a_k_norm': 1.0 + nrm(ks[11], (N_EVEN, HEAD_DIM), 0.02),
        'a_sink': nrm(ks[12], (N_EVEN, A_Q_HEADS), 0.5),
        'b_dw_w': nrm(ks[13], (N_EVEN, CONV_WIDTH, CONV_CH), CONV_WIDTH ** -0.5),
        'b_dw_b': nrm(ks[14], (N_EVEN, CONV_CH), 0.02),
        'b_ln_g': 1.0 + nrm(ks[15], (N_EVEN, CONV_CH), 0.02),
        'b_ln_b': nrm(ks[16], (N_EVEN, CONV_CH), 0.02),
        'c_w_in': nrm(ks[17], (N_ODD, D, C_IN), D ** -0.5),
        'c_q_a_norm': 1.0 + nrm(ks[18], (N_ODD, Q_LORA), 0.02),
        'c_kv_a_norm': 1.0 + nrm(ks[19], (N_ODD, KV_LORA), 0.02),
        'c_w_q_b': nrm(ks[20], (N_ODD, Q_LORA, MLA_HEADS * MLA_QK), Q_LORA ** -0.5),
        'c_w_kv_b': nrm(ks[21], (N_ODD, KV_LORA, MLA_HEADS * (MLA_NOPE + MLA_V)), KV_LORA ** -0.5),
        'c_q_norm': 1.0 + nrm(ks[22], (N_ODD, MLA_QK), 0.02),
        'c_k_norm': 1.0 + nrm(ks[23], (N_ODD, MLA_NOPE), 0.02),
        'c_kr_norm': 1.0 + nrm(ks[24], (N_ODD, MLA_ROPE), 0.02),
        'c_w_out': nrm(ks[25], (N_ODD, MLA_HEADS * MLA_V, D), (MLA_HEADS * MLA_V) ** -0.5),
        'router_w': nrm(ks[26], (DEPTH, D, N_EXPERTS), D ** -0.5),
        'router_b': nrm(ks[27], (DEPTH, N_EXPERTS), 0.01),
        'exp_w_gu': nrm(ks[28], (DEPTH, N_EXPERTS, D, 2 * D_EXPERT), D ** -0.5),
        'exp_b_gu': nrm(ks[29], (DEPTH, N_EXPERTS, 2 * D_EXPERT), 0.01),
        'exp_w_down': nrm(ks[30], (DEPTH, N_EXPERTS, D_EXPERT, D), D_EXPERT ** -0.5),
        'exp_b_down': nrm(ks[31], (DEPTH, N_EXPERTS, D), 0.01),
    }


def reference(x, c, ctx, c_ctx, mod_w, mod_b, norm_mix_g, norm_ffn_g,
              ab_w_in, ab_w_out, a_q_norm, a_k_norm, a_sink, b_dw_w, b_dw_b, b_ln_g, b_ln_b,
              c_w_in, c_q_a_norm, c_kv_a_norm, c_w_q_b, c_w_kv_b, c_q_norm, c_k_norm, c_kr_norm, c_w_out,
              router_w, router_b, exp_w_gu, exp_b_gu, exp_w_down, exp_b_down):
    b, s_len, d = x.shape
    cos_a, sin_a = axial_rope_tables(s_len, HEAD_DIM)
    cos_c, sin_c = axial_rope_tables(s_len, MLA_ROPE)
    silu_c = jax.nn.silu(c)
    silu_cc = jax.nn.silu(c_ctx)
    n_lat = b * s_len
    for layer in range(DEPTH):
        keep_ctx = layer < DEPTH - 1
        mod_l = (silu_c @ mod_w[layer] + mod_b[layer])[:, None, :]
        mod_c = silu_cc @ mod_w[layer] + mod_b[layer]
        sh1, sc1, g1, sh2, sc2, g2 = jnp.split(mod_l, 6, axis=-1)
        csh1, csc1, cg1, csh2, csc2, cg2 = jnp.split(mod_c, 6, axis=-1)
        h_l = modulate(rms_norm(x, norm_mix_g[layer]), sh1, sc1)
        h_c = modulate(rms_norm(ctx, norm_mix_g[layer]), csh1, csc1)
        i = layer // 2
        if layer % 2 == 0:
            y_l, y_c = mixer_window_conv(h_l, h_c, ab_w_in[i], ab_w_out[i], a_q_norm[i], a_k_norm[i], a_sink[i],
                                         b_dw_w[i], b_dw_b[i], b_ln_g[i], b_ln_b[i], cos_a, sin_a, keep_ctx)
        else:
            y_l, y_c = mixer_mla(h_l, h_c, c_w_in[i], c_q_a_norm[i], c_kv_a_norm[i], c_w_q_b[i], c_w_kv_b[i],
                                 c_q_norm[i], c_k_norm[i], c_kr_norm[i], c_w_out[i], cos_c, sin_c, keep_ctx)
        x = x + g1 * y_l
        f_l = modulate(rms_norm(x, norm_ffn_g[layer]), sh2, sc2)
        if keep_ctx:
            ctx = ctx + cg1 * y_c
            f_c = modulate(rms_norm(ctx, norm_ffn_g[layer]), csh2, csc2)
            tokens = jnp.concatenate([f_l.reshape(-1, d), f_c.reshape(-1, d)], axis=0)
            y = moe_ffn(tokens, router_w[layer], router_b[layer], exp_w_gu[layer], exp_b_gu[layer],
                        exp_w_down[layer], exp_b_down[layer])
            x = x + g2 * y[:n_lat].reshape(x.shape)
            ctx = ctx + cg2 * y[n_lat:].reshape(ctx.shape)
        else:
            y = moe_ffn(f_l.reshape(-1, d), router_w[layer], router_b[layer], exp_w_gu[layer], exp_b_gu[layer],
                        exp_w_down[layer], exp_b_down[layer])
            x = x + g2 * y.reshape(x.shape)
    return x
```

```python
import functools

import jax
import jax.numpy as jnp
from jax import lax
from jax.experimental import pallas as pl
from jax.experimental.pallas import tpu as pltpu

D_MODEL = 1024
BATCH = 4
SEQ = 4096
DEPTH = 4
GRID_W = 64
CTX_LEN = 256
A_Q_HEADS = 8
A_KV_HEADS = 2
HEAD_DIM = 64
WINDOW = 128
CONV_CH = 512
CONV_WIDTH = 31
MLA_HEADS = 16
MLA_NOPE = 64
MLA_ROPE = 32
MLA_V = 64
MLA_QK = MLA_NOPE + MLA_ROPE
Q_LORA = 768
KV_LORA = 256
N_EXPERTS = 32
TOP_K = 4
D_EXPERT = 1024
SWIGLU_LIMIT = 7.0
SWIGLU_ALPHA = 1.702
ROPE_THETA = 10000.0
EPS = 1e-6
NEG_INF = -1e30

N_LAT = BATCH * SEQ
N_CTX = BATCH * CTX_LEN
N_TOK = N_LAT + N_CTX
N_SEG = BATCH + 1
SEG_PAD = 8
DQ = A_Q_HEADS * HEAD_DIM
DKV = A_KV_HEADS * HEAD_DIM
MLA_PAD = 128

ROW_TILE = 512
MOE_ROWS = 256
CONV_TILE = 256
CONV_HALO = 16
MLA_Q_TILE = 256
LANE = 128
VMEM_LIMIT = 56 * 1024 * 1024

F32 = jnp.float32
BF16 = jnp.bfloat16


def _params(*sem):
    return pltpu.CompilerParams(dimension_semantics=sem, vmem_limit_bytes=VMEM_LIMIT)


def _seg_of_tile(i, tile):
    return jnp.minimum(i * tile // SEQ, BATCH)


def _mod_kernel(c_ref, w_ref, b_ref, o_ref):
    c = c_ref[...]
    s = c * jax.nn.sigmoid(c)
    o_ref[...] = jnp.dot(s, w_ref[...], precision=lax.Precision.HIGHEST,
                         preferred_element_type=F32) + b_ref[...]


def _modulation(cvec, mod_w, mod_b):
    tn = 1536
    return pl.pallas_call(
        _mod_kernel,
        grid=(DEPTH, 6 * D_MODEL // tn),
        in_specs=[
            pl.BlockSpec((SEG_PAD, D_MODEL), lambda l, j: (0, 0)),
            pl.BlockSpec((None, D_MODEL, tn), lambda l, j: (l, 0, j)),
            pl.BlockSpec((None, 1, tn), lambda l, j: (l, 0, j)),
        ],
        out_specs=pl.BlockSpec((None, SEG_PAD, tn), lambda l, j: (l, 0, j)),
        out_shape=jax.ShapeDtypeStruct((DEPTH, SEG_PAD, 6 * D_MODEL), F32),
        compiler_params=_params("arbitrary", "arbitrary"),
        name="modulation",
    )(cvec, mod_w, mod_b.reshape(DEPTH, 1, 6 * D_MODEL))


def _mod_spec(chunk, tile):
    return pl.BlockSpec((None, None, 1, D_MODEL), lambda i: (_seg_of_tile(i, tile), chunk, 0, 0))


def _rms(x):
    return x * lax.rsqrt(jnp.mean(x * x, axis=-1, keepdims=True) + EPS)


def _inproj_kernel(x_ref, g_ref, sh_ref, sc_ref, w_ref, o_ref):
    h = _rms(x_ref[...]) * g_ref[...]
    h = h * (1.0 + sc_ref[...]) + sh_ref[...]
    o_ref[...] = jnp.dot(h.astype(BF16), w_ref[...], preferred_element_type=F32)


def _inproj(x, g, mods4, w_bf):
    n = w_bf.shape[1]
    t = x.shape[0]
    return pl.pallas_call(
        _inproj_kernel,
        grid=(t // ROW_TILE,),
        in_specs=[
            pl.BlockSpec((ROW_TILE, D_MODEL), lambda i: (i, 0)),
            pl.BlockSpec((1, D_MODEL), lambda i: (0, 0)),
            _mod_spec(0, ROW_TILE),
            _mod_spec(1, ROW_TILE),
            pl.BlockSpec((D_MODEL, n), lambda i: (0, 0)),
        ],
        out_specs=pl.BlockSpec((ROW_TILE, n), lambda i: (i, 0)),
        out_shape=jax.ShapeDtypeStruct((t, n), F32),
        compiler_params=_params("parallel"),
        name="inproj",
    )(x, g.reshape(1, D_MODEL), mods4, mods4, w_bf)


def _mm_kernel(a_ref, w_ref, o_ref):
    o_ref[...] = jnp.dot(a_ref[...], w_ref[...], preferred_element_type=F32)


def _matmul(a_bf, w_bf):
    t, k = a_bf.shape
    n = w_bf.shape[1]
    return pl.pallas_call(
        _mm_kernel,
        grid=(t // ROW_TILE,),
        in_specs=[pl.BlockSpec((ROW_TILE, k), lambda i: (i, 0)),
                  pl.BlockSpec((k, n), lambda i: (0, 0))],
        out_specs=pl.BlockSpec((ROW_TILE, n), lambda i: (i, 0)),
        out_shape=jax.ShapeDtypeStruct((t, n), F32),
        compiler_params=_params("parallel"),
        name="matmul",
    )(a_bf, w_bf)


def _outproj_kernel(n_in, *refs):
    a_refs = refs[:n_in]
    w_refs = refs[n_in:2 * n_in]
    x_ref, g1_ref, gn_ref, sh_ref, sc_ref, rw_ref, rb_ref, xo_ref, f_ref, lg_ref = refs[2 * n_in:]
    y = jnp.dot(a_refs[0][...], w_refs[0][...], preferred_element_type=F32)
    for a_ref, w_ref in zip(a_refs[1:], w_refs[1:]):
        y = y + jnp.dot(a_ref[...], w_ref[...], preferred_element_type=F32)
    xn = x_ref[...] + g1_ref[...] * y
    xo_ref[...] = xn
    f = _rms(xn) * gn_ref[...]
    f = f * (1.0 + sc_ref[...]) + sh_ref[...]
    f_ref[...] = f.astype(BF16)
    lg_ref[...] = jnp.dot(f, rw_ref[...], precision=lax.Precision.HIGHEST,
                          preferred_element_type=F32) + rb_ref[...]


def _outproj(a_list, w_list, x, mods4, gn, rw_pad, rb_pad):
    t = x.shape[0]
    n_in = len(a_list)
    in_specs = [pl.BlockSpec((ROW_TILE, a.shape[1]), lambda i: (i, 0)) for a in a_list]
    in_specs += [pl.BlockSpec(w.shape, lambda i: (0, 0)) for w in w_list]
    in_specs += [
        pl.BlockSpec((ROW_TILE, D_MODEL), lambda i: (i, 0)),
        _mod_spec(2, ROW_TILE),
        pl.BlockSpec((1, D_MODEL), lambda i: (0, 0)),
        _mod_spec(3, ROW_TILE),
        _mod_spec(4, ROW_TILE),
        pl.BlockSpec((D_MODEL, LANE), lambda i: (0, 0)),
        pl.BlockSpec((1, LANE), lambda i: (0, 0)),
    ]
    return pl.pallas_call(
        functools.partial(_outproj_kernel, n_in),
        grid=(t // ROW_TILE,),
        in_specs=in_specs,
        out_specs=[
            pl.BlockSpec((ROW_TILE, D_MODEL), lambda i: (i, 0)),
            pl.BlockSpec((ROW_TILE, D_MODEL), lambda i: (i, 0)),
            pl.BlockSpec((ROW_TILE, LANE), lambda i: (i, 0)),
        ],
        out_shape=[
            jax.ShapeDtypeStruct((t, D_MODEL), F32),
            jax.ShapeDtypeStruct((t, D_MODEL), BF16),
            jax.ShapeDtypeStruct((t, LANE), F32),
        ],
        compiler_params=_params("parallel"),
        name="outproj",
    )(*a_list, *w_list, x, mods4, gn.reshape(1, D_MODEL), mods4, mods4, rw_pad, rb_pad)


def _softmax_pv(s, sk, v):
    m = jnp.maximum(jnp.max(s, axis=-1, keepdims=True), sk)
    p = jnp.exp(s - m)
    l = jnp.sum(p, axis=-1, keepdims=True) + jnp.exp(sk - m)
    o = jnp.dot(p.astype(BF16), v, preferred_element_type=F32)
    return o / l


def _nt_dot(a, b):
    return lax.dot_general(a, b, (((1,), (1,)), ((), ())), preferred_element_type=F32)


def _win_attn_kernel(q_ref, kp_ref, kc_ref, kn_ref, vp_ref, vc_ref, vn_ref, kx_ref, vx_ref,
                     sink_ref, o_ref):
    n = pl.program_id(1)
    w = WINDOW
    k_all = jnp.concatenate([kp_ref[...], kc_ref[...], kn_ref[...], kx_ref[...]], axis=0)
    v_all = jnp.concatenate([vp_ref[...], vc_ref[...], vn_ref[...], vx_ref[...]], axis=0)
    nk = 3 * w + CTX_LEN
    qi = lax.broadcasted_iota(jnp.int32, (w, nk), 0)
    ki = lax.broadcasted_iota(jnp.int32, (w, nk), 1)
    rel = ki - w - qi
    kpos = (n - 1) * w + ki
    valid = (ki >= 3 * w) | ((jnp.abs(rel) <= w) & (kpos >= 0) & (kpos < SEQ))
    for h in range(A_Q_HEADS):
        hk = h // (A_Q_HEADS // A_KV_HEADS)
        qh = q_ref[:, h * HEAD_DIM:(h + 1) * HEAD_DIM]
        kh = k_all[:, hk * HEAD_DIM:(hk + 1) * HEAD_DIM]
        vh = v_all[:, hk * HEAD_DIM:(hk + 1) * HEAD_DIM]
        s = jnp.where(valid, _nt_dot(qh, kh), NEG_INF)
        o = _softmax_pv(s, sink_ref[h:h + 1, 0:1], vh)
        o_ref[:, h * HEAD_DIM:(h + 1) * HEAD_DIM] = o.astype(o_ref.dtype)


def _win_attn(q, k, v, kx, vx, sink_b):
    nb = SEQ // WINDOW
    kv_spec = lambda f: pl.BlockSpec((None, WINDOW, DKV), f)
    prev = lambda b, n: (b, jnp.maximum(n - 1, 0), 0)
    cur = lambda b, n: (b, n, 0)
    nxt = lambda b, n: (b, jnp.minimum(n + 1, nb - 1), 0)
    ctx_spec = pl.BlockSpec((None, CTX_LEN, DKV), lambda b, n: (b, 0, 0))
    return pl.pallas_call(
        _win_attn_kernel,
        grid=(BATCH, nb),
        in_specs=[pl.BlockSpec((None, WINDOW, DQ), cur),
                  kv_spec(prev), kv_spec(cur), kv_spec(nxt),
                  kv_spec(prev), kv_spec(cur), kv_spec(nxt),
                  ctx_spec, ctx_spec,
                  pl.BlockSpec((A_Q_HEADS, LANE), lambda b, n: (0, 0))],
        out_specs=pl.BlockSpec((None, WINDOW, DQ), cur),
        out_shape=jax.ShapeDtypeStruct((BATCH, SEQ, DQ), BF16),
        compiler_params=_params("parallel", "parallel"),
        name="win_attn",
    )(q, k, k, k, v, v, v, kx, vx, sink_b)


def _ctx_attn_kernel(q_ref, k_ref, v_ref, sink_ref, o_ref):
    for h in range(A_Q_HEADS):
        hk = h // (A_Q_HEADS // A_KV_HEADS)
        qh = q_ref[:, h * HEAD_DIM:(h + 1) * HEAD_DIM]
        kh = k_ref[:, hk * HEAD_DIM:(hk + 1) * HEAD_DIM]
        vh = v_ref[:, hk * HEAD_DIM:(hk + 1) * HEAD_DIM]
        o = _softmax_pv(_nt_dot(qh, kh), sink_ref[h:h + 1, 0:1], vh)
        o_ref[:, h * HEAD_DIM:(h + 1) * HEAD_DIM] = o.astype(o_ref.dtype)


def _ctx_attn(q, k, v, sink_b):
    return pl.pallas_call(
        _ctx_attn_kernel,
        grid=(BATCH,),
        in_specs=[pl.BlockSpec((None, CTX_LEN, DQ), lambda b: (b, 0, 0)),
                  pl.BlockSpec((None, CTX_LEN, DKV), lambda b: (b, 0, 0)),
                  pl.BlockSpec((None, CTX_LEN, DKV), lambda b: (b, 0, 0)),
                  pl.BlockSpec((A_Q_HEADS, LANE), lambda b: (0, 0))],
        out_specs=pl.BlockSpec((None, CTX_LEN, DQ), lambda b: (b, 0, 0)),
        out_shape=jax.ShapeDtypeStruct((BATCH, CTX_LEN, DQ), BF16),
        compiler_params=_params("parallel"),
        name="ctx_attn",
    )(q, k, v, sink_b)


def _glu(u):
    return u[:, :CONV_CH] * jax.nn.sigmoid(u[:, CONV_CH:])


def _conv_kernel(up_ref, uc_ref, un_ref, w_ref, b_ref, g_ref, bb_ref, o_ref, zp_ref):
    c = pl.program_id(1)
    nc = pl.num_programs(1)
    tl = CONV_TILE
    h = CONV_HALO
    zp_ref[0:h, :] = jnp.where(c > 0, _glu(up_ref[...]), 0.0)
    zp_ref[h:h + tl, :] = _glu(uc_ref[...])
    zp_ref[h + tl:2 * h + tl, :] = jnp.where(c < nc - 1, _glu(un_ref[...]), 0.0)
    off = h - CONV_WIDTH // 2
    acc = zp_ref[off:off + tl, :] * w_ref[0:1, :]
    for j in range(1, CONV_WIDTH):
        acc = acc + zp_ref[off + j:off + j + tl, :] * w_ref[j:j + 1, :]
    z = acc + b_ref[...]
    mu = jnp.mean(z, axis=-1, keepdims=True)
    zc = z - mu
    y = zc * lax.rsqrt(jnp.mean(zc * zc, axis=-1, keepdims=True) + EPS)
    y = y * g_ref[...] + bb_ref[...]
    o_ref[...] = (y * jax.nn.sigmoid(y)).astype(o_ref.dtype)


def _conv_module(u, dw_w, dw_b, ln_g, ln_b):
    n_seq, length, _ = u.shape
    nc = length // CONV_TILE
    r = CONV_TILE // CONV_HALO
    n_halo = length // CONV_HALO
    halo = lambda f: pl.BlockSpec((None, CONV_HALO, 2 * CONV_CH), f)
    vec = pl.BlockSpec((1, CONV_CH), lambda s, c: (0, 0))
    return pl.pallas_call(
        _conv_kernel,
        grid=(n_seq, nc),
        in_specs=[halo(lambda s, c: (s, jnp.maximum(c * r - 1, 0), 0)),
                  pl.BlockSpec((None, CONV_TILE, 2 * CONV_CH), lambda s, c: (s, c, 0)),
                  halo(lambda s, c: (s, jnp.minimum((c + 1) * r, n_halo - 1), 0)),
                  pl.BlockSpec((CONV_WIDTH, CONV_CH), lambda s, c: (0, 0)),
                  vec, vec, vec],
        out_specs=pl.BlockSpec((None, CONV_TILE, CONV_CH), lambda s, c: (s, c, 0)),
        out_shape=jax.ShapeDtypeStruct((n_seq, length, CONV_CH), BF16),
        scratch_shapes=[pltpu.VMEM((CONV_TILE + 2 * CONV_HALO, CONV_CH), F32)],
        compiler_params=_params("parallel", "parallel"),
        name="conv_module",
    )(u, u, u, dw_w, dw_b.reshape(1, -1), ln_g.reshape(1, -1), ln_b.reshape(1, -1))


def _mla_kernel(q_ref, k_ref, v_ref, o_ref):
    outs = []
    for j in range(2):
        qh = q_ref[:, j * MLA_PAD:(j + 1) * MLA_PAD]
        kh = k_ref[:, j * MLA_PAD:(j + 1) * MLA_PAD]
        s = _nt_dot(qh, kh)
        m = jnp.max(s, axis=-1, keepdims=True)
        p = jnp.exp(s - m)
        l = jnp.sum(p, axis=-1, keepdims=True)
        outs.append(jnp.dot(p.astype(BF16), v_ref[...], preferred_element_type=F32) / l)
    lane = lax.broadcasted_iota(jnp.int32, outs[0].shape, 1)
    o_ref[...] = jnp.where(lane < MLA_V, outs[0], outs[1]).astype(o_ref.dtype)


def _mla_attn(q, k, v):
    _, lq, _ = q.shape
    lk = k.shape[1]
    tq = min(MLA_Q_TILE, lq)
    return pl.pallas_call(
        _mla_kernel,
        grid=(BATCH, MLA_HEADS // 2, lq // tq),
        in_specs=[pl.BlockSpec((None, tq, 2 * MLA_PAD), lambda b, p, i: (b, i, p)),
                  pl.BlockSpec((None, lk, 2 * MLA_PAD), lambda b, p, i: (b, 0, p)),
                  pl.BlockSpec((None, lk, 2 * MLA_V), lambda b, p, i: (b, 0, p))],
        out_specs=pl.BlockSpec((None, tq, 2 * MLA_V), lambda b, p, i: (b, i, p)),
        out_shape=jax.ShapeDtypeStruct((BATCH, lq, MLA_HEADS * MLA_V), BF16),
        compiler_params=_params("parallel", "parallel", "parallel"),
        name="mla_attn",
    )(q, k, v)


def _moe_kernel(be_ref, nu_ref, x_ref, wgu_ref, bgu_ref, wd_ref, bd_ref, o_ref, wgu_s, wd_s):
    i = pl.program_id(0)
    e = be_ref[i]
    e_prev = be_ref[jnp.maximum(i - 1, 0)]

    @pl.when((i == 0) | (e != e_prev))
    def _():
        wgu_s[...] = wgu_ref[...].astype(BF16)
        wd_s[...] = wd_ref[...].astype(BF16)

    @pl.when(i < nu_ref[0])
    def _():
        gu = jnp.dot(x_ref[...], wgu_s[...], preferred_element_type=F32) + bgu_ref[...]
        gate = jnp.minimum(gu[:, :D_EXPERT], SWIGLU_LIMIT)
        up = jnp.clip(gu[:, D_EXPERT:], -SWIGLU_LIMIT, SWIGLU_LIMIT)
        act = (up + 1.0) * (gate * jax.nn.sigmoid(SWIGLU_ALPHA * gate))
        o_ref[...] = jnp.dot(act.astype(BF16), wd_s[...], preferred_element_type=F32) + bd_ref[...]


def _moe_experts(block_expert, n_used, xb, w_gu, b_gu, w_down, b_down):
    n_rows = xb.shape[0]
    n_blocks = n_rows // MOE_ROWS
    grid_spec = pltpu.PrefetchScalarGridSpec(
        num_scalar_prefetch=2,
        grid=(n_blocks,),
        in_specs=[
            pl.BlockSpec((MOE_ROWS, D_MODEL), lambda i, be, nu: (i, 0)),
            pl.BlockSpec((None, D_MODEL, 2 * D_EXPERT), lambda i, be, nu: (be[i], 0, 0)),
            pl.BlockSpec((None, 1, 2 * D_EXPERT), lambda i, be, nu: (be[i], 0, 0)),
            pl.BlockSpec((None, D_EXPERT, D_MODEL), lambda i, be, nu: (be[i], 0, 0)),
            pl.BlockSpec((None, 1, D_MODEL), lambda i, be, nu: (be[i], 0, 0)),
        ],
        out_specs=pl.BlockSpec((MOE_ROWS, D_MODEL), lambda i, be, nu: (i, 0)),
        scratch_shapes=[pltpu.VMEM((D_MODEL, 2 * D_EXPERT), BF16),
                        pltpu.VMEM((D_EXPERT, D_MODEL), BF16)],
    )
    return pl.pallas_call(
        _moe_kernel,
        grid_spec=grid_spec,
        out_shape=jax.ShapeDtypeStruct((n_rows, D_MODEL), F32),
        compiler_params=_params("arbitrary"),
        name="moe_experts",
    )(block_expert, n_used, xb, w_gu, b_gu.reshape(N_EXPERTS, 1, -1), w_down,
      b_down.reshape(N_EXPERTS, 1, -1))


def _moe(f_bf, logits, w_gu, b_gu, w_down, b_down):
    t = f_bf.shape[0]
    top_val, top_idx = lax.top_k(logits, TOP_K)
    gates = jax.nn.softmax(top_val, axis=-1)
    onehot = jnp.sum((top_idx[:, :, None] == jnp.arange(N_EXPERTS)[None, None, :]).astype(jnp.int32), axis=1)
    cum = jnp.cumsum(onehot, axis=0)
    counts = cum[-1]
    pos = jnp.take_along_axis(cum - onehot, top_idx, axis=1)
    padded = (counts + MOE_ROWS - 1) // MOE_ROWS * MOE_ROWS
    pad_ends = jnp.cumsum(padded)
    pad_starts = pad_ends - padded
    dest = pad_starts[top_idx] + pos
    n_blocks = t * TOP_K // MOE_ROWS + N_EXPERTS
    tok = jnp.broadcast_to(jnp.arange(t, dtype=jnp.int32)[:, None], (t, TOP_K))
    row_tok = jnp.zeros((n_blocks * MOE_ROWS,), jnp.int32).at[dest.reshape(-1)].set(tok.reshape(-1))
    block_start = jnp.arange(n_blocks, dtype=jnp.int32) * MOE_ROWS
    block_expert = jnp.minimum(jnp.searchsorted(pad_ends, block_start, side='right'),
                               N_EXPERTS - 1).astype(jnp.int32)
    n_used = (pad_ends[-1:] // MOE_ROWS).astype(jnp.int32)
    xb = f_bf[row_tok]
    yb = _moe_experts(block_expert, n_used, xb, w_gu, b_gu, w_down, b_down)
    return jnp.sum(yb[dest] * gates[:, :, None], axis=1)


def _rope_tables(rot_dims):
    rows = SEQ // GRID_W
    row = jnp.broadcast_to(jnp.arange(rows)[:, None], (rows, GRID_W)).reshape(-1).astype(F32)
    col = jnp.broadcast_to(jnp.arange(GRID_W)[None, :], (rows, GRID_W)).reshape(-1).astype(F32)
    axis_dims = rot_dims // 2
    inv = ROPE_THETA ** (-(jnp.arange(axis_dims // 2, dtype=F32) * 2.0) / axis_dims)
    ang = jnp.concatenate([row[:, None] * inv, col[:, None] * inv], axis=-1)
    return jnp.cos(ang), jnp.sin(ang)


def _rope(x, cos, sin):
    n = x.shape[-1] // 4
    xf = x.reshape(x.shape[:-1] + (2, 2, n))
    c = cos.reshape(-1, 1, 2, n)
    s = sin.reshape(-1, 1, 2, n)
    x1 = xf[..., 0, :]
    x2 = xf[..., 1, :]
    return jnp.stack([x1 * c - x2 * s, x2 * c + x1 * s], axis=-2).reshape(x.shape)


def _rms_g(x, g):
    return _rms(x) * g


def _split_tokens(p):
    return p[:N_LAT].reshape(BATCH, SEQ, -1), p[N_LAT:].reshape(BATCH, CTX_LEN, -1)


def _join_tokens(lat, ctx):
    return jnp.concatenate([lat.reshape(N_LAT, -1), ctx.reshape(N_CTX, -1)], axis=0)


def _mixer_window_conv(xt, g, mods4, w_in, q_g, k_g, sink, dw_w, dw_b, ln_g, ln_b, cos, sin):
    p = _inproj(xt, g, mods4, w_in.astype(BF16))
    heads = lambda a, nh: a.reshape(a.shape[0], nh, HEAD_DIM)
    q = _rms_g(heads(p[:, :DQ], A_Q_HEADS), q_g)
    k = _rms_g(heads(p[:, DQ:DQ + DKV], A_KV_HEADS), k_g)
    v = p[:, DQ + DKV:DQ + 2 * DKV]
    u = p[:, DQ + 2 * DKV:]
    q_l, q_c = _split_tokens(q.reshape(N_TOK, DQ))
    k_l, k_c = _split_tokens(k.reshape(N_TOK, DKV))
    v_l, v_c = _split_tokens(v)
    u_l, u_c = _split_tokens(u)
    q_l = _rope(q_l.reshape(BATCH, SEQ, A_Q_HEADS, HEAD_DIM), cos, sin).reshape(BATCH, SEQ, DQ)
    k_l = _rope(k_l.reshape(BATCH, SEQ, A_KV_HEADS, HEAD_DIM), cos, sin).reshape(BATCH, SEQ, DKV)
    scale = HEAD_DIM ** -0.5
    sink_b = jnp.broadcast_to(sink.astype(F32)[:, None], (A_Q_HEADS, LANE))
    bf = lambda a: a.astype(BF16)
    a_l = _win_attn(bf(q_l * scale), bf(k_l), bf(v_l), bf(k_c), bf(v_c), sink_b)
    a_c = _ctx_attn(bf(q_c * scale), bf(k_c), bf(v_c), sink_b)
    b_l = _conv_module(u_l, dw_w, dw_b, ln_g, ln_b)
    b_c = _conv_module(u_c, dw_w, dw_b, ln_g, ln_b)
    return _join_tokens(a_l, a_c), _join_tokens(b_l, b_c)


def _mixer_mla(xt, g, mods4, w_in, q_a_g, kv_a_g, w_q_b, w_kv_b, q_g, k_g, kr_g, cos, sin, with_ctx_out):
    p = _inproj(xt, g, mods4, w_in.astype(BF16))
    cq = _rms_g(p[:, :Q_LORA], q_a_g)
    ckv = _rms_g(p[:, Q_LORA:Q_LORA + KV_LORA], kv_a_g)
    kr = _rms_g(p[:, Q_LORA + KV_LORA:], kr_g)
    q = _matmul(cq.astype(BF16), w_q_b.astype(BF16)).reshape(N_TOK, MLA_HEADS, MLA_QK)
    q = _rms_g(q, q_g)
    kv = _matmul(ckv.astype(BF16), w_kv_b.astype(BF16)).reshape(N_TOK, MLA_HEADS, MLA_NOPE + MLA_V)
    kn = _rms_g(kv[..., :MLA_NOPE], k_g)
    v = kv[..., MLA_NOPE:].reshape(N_TOK, MLA_HEADS * MLA_V)
    q_l, q_c = _split_tokens(q.reshape(N_TOK, -1))
    kr_l, kr_c = _split_tokens(kr)
    q_l = q_l.reshape(BATCH, SEQ, MLA_HEADS, MLA_QK)
    q_l = jnp.concatenate([q_l[..., :MLA_NOPE], _rope(q_l[..., MLA_NOPE:], cos, sin)], axis=-1)
    kr_l = _rope(kr_l[:, :, None, :], cos, sin)[:, :, 0, :]
    scale = MLA_QK ** -0.5
    pad_q = lambda a: jnp.pad(a * scale, ((0, 0), (0, 0), (0, 0), (0, MLA_PAD - MLA_QK))).astype(BF16)
    qp_l = pad_q(q_l).reshape(BATCH, SEQ, MLA_HEADS * MLA_PAD)
    qp_c = pad_q(q_c.reshape(BATCH, CTX_LEN, MLA_HEADS, MLA_QK)).reshape(BATCH, CTX_LEN, MLA_HEADS * MLA_PAD)
    kn_l, kn_c = _split_tokens(kn.reshape(N_TOK, -1))
    v_l, v_c = _split_tokens(v)

    def pack_k(kn_x, kr_x):
        b, l = kr_x.shape[:2]
        kn_x = kn_x.reshape(b, l, MLA_HEADS, MLA_NOPE)
        kr_x = jnp.broadcast_to(kr_x[:, :, None, :], (b, l, MLA_HEADS, MLA_ROPE))
        z = jnp.zeros((b, l, MLA_HEADS, MLA_PAD - MLA_QK), F32)
        return jnp.concatenate([kn_x, kr_x, z], axis=-1).reshape(b, l, MLA_HEADS * MLA_PAD).astype(BF16)

    kp_l = pack_k(kn_l, kr_l)
    kp_c = pack_k(kn_c, kr_c)
    k_all = jnp.concatenate([kp_l, kp_c], axis=1)
    v_all = jnp.concatenate([v_l, v_c], axis=1).astype(BF16)
    o_l = _mla_attn(qp_l, k_all, v_all)
    if with_ctx_out:
        o_c = _mla_attn(qp_c, kp_c, v_c.astype(BF16))
    else:
        o_c = jnp.zeros((BATCH, CTX_LEN, MLA_HEADS * MLA_V), BF16)
    return _join_tokens(o_l, o_c)


def kernel(x, c, ctx, c_ctx, mod_w, mod_b, norm_mix_g, norm_ffn_g, ab_w_in, ab_w_out, a_q_norm, a_k_norm, a_sink, b_dw_w, b_dw_b, b_ln_g, b_ln_b, c_w_in, c_q_a_norm, c_kv_a_norm, c_w_q_b, c_w_kv_b, c_q_norm, c_k_norm, c_kr_norm, c_w_out, router_w, router_b, exp_w_gu, exp_b_gu, exp_w_down, exp_b_down):
    cos_a, sin_a = _rope_tables(HEAD_DIM)
    cos_c, sin_c = _rope_tables(MLA_ROPE)
    cvec = jnp.concatenate([c, c_ctx[None, :], jnp.zeros((SEG_PAD - N_SEG, D_MODEL), F32)], axis=0)
    mods = _modulation(cvec, mod_w, mod_b)
    xt = jnp.concatenate([x.reshape(N_LAT, D_MODEL), ctx.reshape(N_CTX, D_MODEL)], axis=0)
    for layer in range(DEPTH):
        keep_ctx = layer < DEPTH - 1
        mods4 = mods[layer].reshape(SEG_PAD, 6, 1, D_MODEL)
        i = layer // 2
        if layer % 2 == 0:
            a, b = _mixer_window_conv(xt, norm_mix_g[layer], mods4, ab_w_in[i], a_q_norm[i], a_k_norm[i],
                                      a_sink[i], b_dw_w[i], b_dw_b[i], b_ln_g[i], b_ln_b[i], cos_a, sin_a)
            w_out = ab_w_out[i].astype(BF16)
            a_list, w_list = [a, b], [w_out[:DQ], w_out[DQ:]]
        else:
            o = _mixer_mla(xt, norm_mix_g[layer], mods4, c_w_in[i], c_q_a_norm[i], c_kv_a_norm[i], c_w_q_b[i],
                           c_w_kv_b[i], c_q_norm[i], c_k_norm[i], c_kr_norm[i], cos_c, sin_c, keep_ctx)
            a_list, w_list = [o], [c_w_out[i].astype(BF16)]
        rw_pad = jnp.pad(router_w[layer], ((0, 0), (0, LANE - N_EXPERTS)))
        rb_pad = jnp.pad(router_b[layer], (0, LANE - N_EXPERTS)).reshape(1, LANE)
        xt, f_bf, logits = _outproj(a_list, w_list, xt, mods4, norm_ffn_g[layer], rw_pad, rb_pad)
        n_moe = N_TOK if keep_ctx else N_LAT
        y = _moe(f_bf[:n_moe], logits[:n_moe, :N_EXPERTS], exp_w_gu[layer], exp_b_gu[layer],
                 exp_w_down[layer], exp_b_down[layer])
        g2 = mods4[:, 5, 0, :]
        x_lat = xt[:N_LAT].reshape(BATCH, SEQ, D_MODEL) + g2[:BATCH, None, :] * y[:N_LAT].reshape(BATCH, SEQ, D_MODEL)
        if keep_ctx:
            x_ctx = xt[N_LAT:] + g2[BATCH][None, :] * y[N_LAT:]
        else:
            x_ctx = xt[N_LAT:]
        xt = jnp.concatenate([x_lat.reshape(N_LAT, D_MODEL), x_ctx], axis=0)
    return xt[:N_LAT].reshape(BATCH, SEQ, D_MODEL)
```

```python
import functools

import jax
import jax.numpy as jnp
import numpy as np
from jax import lax
from jax.experimental import pallas as pl
from jax.experimental.pallas import tpu as pltpu

D_MODEL = 1024
BATCH = 4
SEQ = 4096
DEPTH = 4
GRID_W = 64
CTX_LEN = 256
A_Q_HEADS = 8
A_KV_HEADS = 2
HEAD_DIM = 64
WINDOW = 128
CONV_CH = 512
CONV_WIDTH = 31
MLA_HEADS = 16
MLA_NOPE = 64
MLA_ROPE = 32
MLA_V = 64
MLA_QK = MLA_NOPE + MLA_ROPE
Q_LORA = 768
KV_LORA = 256
N_EXPERTS = 32
TOP_K = 4
D_EXPERT = 1024
SWIGLU_LIMIT = 7.0
SWIGLU_ALPHA = 1.702
ROPE_THETA = 10000.0
EPS = 1e-6
NEG_INF = -1e30

LANE = 128
N_LAT = BATCH * SEQ
N_CTX = BATCH * CTX_LEN
N_TOK = N_LAT + N_CTX
N_SEG = BATCH + 1
SEG_PAD = 8
DQ = A_Q_HEADS * HEAD_DIM
DKV = A_KV_HEADS * HEAD_DIM
AB_IN = DQ + 2 * DKV + 2 * CONV_CH
MLA_PAD = LANE
C_IN_PAD = Q_LORA + KV_LORA + LANE
MLA_KEYS = SEQ + CTX_LEN

TOK_TILE = 256
LAT_TILES = N_LAT // TOK_TILE
TILES_PER_SEQ = SEQ // TOK_TILE
OUT_TILE = 512
MOE_ROWS = 256
CONV_HALO = 16
VMEM_LIMIT = 56 * 1024 * 1024

F32 = jnp.float32
BF16 = jnp.bfloat16


def _params(*sem):
    return pltpu.CompilerParams(dimension_semantics=sem, vmem_limit_bytes=VMEM_LIMIT)


def _seg_of_tile(i, tile):
    return jnp.minimum(i * tile // SEQ, BATCH)


def _rope_block_of_tile(i):
    return jnp.where(i < LAT_TILES, i % TILES_PER_SEQ, TILES_PER_SEQ)


def _mod_spec(chunk, tile):
    return pl.BlockSpec((None, None, 1, D_MODEL), lambda i: (_seg_of_tile(i, tile), chunk, 0, 0))


def _full(shape):
    return pl.BlockSpec(shape, lambda *_: (0,) * len(shape))


def _rms(x, n=None):
    n = x.shape[-1] if n is None else n
    return x * lax.rsqrt(jnp.sum(x * x, axis=-1, keepdims=True) * (1.0 / n) + EPS)


def _modulated_norm(x_ref, g_ref, sh_ref, sc_ref):
    h = _rms(x_ref[...]) * g_ref[...]
    return h * (1.0 + sc_ref[...]) + sh_ref[...]


def _mod_kernel(c_ref, w_ref, b_ref, o_ref):
    c = c_ref[...]
    s = c * jax.nn.sigmoid(c)
    o_ref[...] = jnp.dot(s, w_ref[...], precision=lax.Precision.HIGHEST,
                         preferred_element_type=F32) + b_ref[...]


def _modulation(cvec, mod_w, mod_b):
    tn = 1536
    return pl.pallas_call(
        _mod_kernel,
        grid=(DEPTH, 6 * D_MODEL // tn),
        in_specs=[
            pl.BlockSpec((SEG_PAD, D_MODEL), lambda l, j: (0, 0)),
            pl.BlockSpec((None, D_MODEL, tn), lambda l, j: (l, 0, j)),
            pl.BlockSpec((None, 1, tn), lambda l, j: (l, 0, j)),
        ],
        out_specs=pl.BlockSpec((None, SEG_PAD, tn), lambda l, j: (l, 0, j)),
        out_shape=jax.ShapeDtypeStruct((DEPTH, SEG_PAD, 6 * D_MODEL), F32),
        compiler_params=_params("arbitrary", "arbitrary"),
        name="modulation",
    )(cvec, mod_w, mod_b.reshape(DEPTH, 1, 6 * D_MODEL))


def _rope_lanes(t, cos, sin, half0, shift):
    partner = jnp.where(half0, pltpu.roll(t, LANE - shift, 1), pltpu.roll(t, shift, 1))
    return t * cos + partner * sin


def _rope_lane_tables(rot_dims, lane_start, repeat):
    rows = SEQ // GRID_W
    row = jnp.broadcast_to(jnp.arange(rows)[:, None], (rows, GRID_W)).reshape(-1).astype(F32)
    col = jnp.broadcast_to(jnp.arange(GRID_W)[None, :], (rows, GRID_W)).reshape(-1).astype(F32)
    axis_dims = rot_dims // 2
    n = axis_dims // 2
    inv = ROPE_THETA ** (-(jnp.arange(n, dtype=F32) * 2.0) / axis_dims)
    ang = jnp.concatenate([row[:, None] * inv, col[:, None] * inv], axis=-1)
    d = np.arange(rot_dims)
    src = (d // (2 * n)) * n + d % n
    sign = np.where((d % (2 * n)) // n == 0, -1.0, 1.0).astype(np.float32)
    cos_blk = jnp.cos(ang)[:, src]
    sin_blk = jnp.sin(ang)[:, src] * sign
    cos_t = jnp.ones((SEQ + TOK_TILE, LANE), F32)
    sin_t = jnp.zeros((SEQ + TOK_TILE, LANE), F32)
    for r in range(repeat):
        lo = lane_start + r * rot_dims
        cos_t = cos_t.at[:SEQ, lo:lo + rot_dims].set(cos_blk)
        sin_t = sin_t.at[:SEQ, lo:lo + rot_dims].set(sin_blk)
    return cos_t, sin_t


def _even_inproj_kernel(x_ref, g_ref, sh_ref, sc_ref, w_ref, qg_ref, kg_ref, cos_ref, sin_ref,
                        q_ref, k_ref, v_ref, z_ref):
    h = _modulated_norm(x_ref, g_ref, sh_ref, sc_ref)
    p = jnp.dot(h.astype(BF16), w_ref[...], preferred_element_type=F32)
    lane = lax.broadcasted_iota(jnp.int32, (TOK_TILE, LANE), 1)
    lo = lane < HEAD_DIM
    half0 = (lane & (HEAD_DIM // 2 - 1)) < HEAD_DIM // 4
    cos = cos_ref[...]
    sin = sin_ref[...]

    def norm_rope(t, gvec):
        ss = t * t
        s_lo = jnp.sum(jnp.where(lo, ss, 0.0), axis=-1, keepdims=True)
        s_hi = jnp.sum(jnp.where(lo, 0.0, ss), axis=-1, keepdims=True)
        ms = jnp.where(lo, s_lo, s_hi) * (1.0 / HEAD_DIM)
        tn = t * lax.rsqrt(ms + EPS) * gvec
        return _rope_lanes(tn, cos, sin, half0, HEAD_DIM // 4)

    scale = HEAD_DIM ** -0.5
    for j in range(DQ // LANE):
        q_ref[:, j * LANE:(j + 1) * LANE] = (norm_rope(p[:, j * LANE:(j + 1) * LANE], qg_ref[...]) * scale).astype(BF16)
    k_ref[...] = norm_rope(p[:, DQ:DQ + DKV], kg_ref[...]).astype(BF16)
    v_ref[...] = p[:, DQ + DKV:DQ + 2 * DKV].astype(BF16)
    u0 = DQ + 2 * DKV
    z_ref[...] = (p[:, u0:u0 + CONV_CH] * jax.nn.sigmoid(p[:, u0 + CONV_CH:])).astype(BF16)


def _even_inproj(xt, g, mods4, w_bf, qg2, kg2, cos_t, sin_t):
    row = lambda n: pl.BlockSpec((TOK_TILE, n), lambda i: (i, 0))
    tab = pl.BlockSpec((TOK_TILE, LANE), lambda i: (_rope_block_of_tile(i), 0))
    return pl.pallas_call(
        _even_inproj_kernel,
        grid=(N_TOK // TOK_TILE,),
        in_specs=[row(D_MODEL), _full((1, D_MODEL)), _mod_spec(0, TOK_TILE), _mod_spec(1, TOK_TILE),
                  _full((D_MODEL, AB_IN)), _full((1, LANE)), _full((1, LANE)), tab, tab],
        out_specs=[row(DQ), row(DKV), row(DKV), row(CONV_CH)],
        out_shape=[jax.ShapeDtypeStruct((N_TOK, DQ), BF16),
                   jax.ShapeDtypeStruct((N_TOK, DKV), BF16),
                   jax.ShapeDtypeStruct((N_TOK, DKV), BF16),
                   jax.ShapeDtypeStruct((N_TOK, CONV_CH), BF16)],
        compiler_params=_params("parallel"),
        name="even_inproj",
    )(xt, g.reshape(1, D_MODEL), mods4, mods4, w_bf, qg2, kg2, cos_t, sin_t)


def _softmax_pv(s, sk, v):
    m = jnp.maximum(jnp.max(s, axis=-1, keepdims=True), sk)
    p = jnp.exp(s - m)
    l = jnp.sum(p, axis=-1, keepdims=True) + jnp.exp(sk - m)
    o = jnp.dot(p.astype(BF16), v, preferred_element_type=F32)
    return o / l


def _nt_dot(a, b):
    return lax.dot_general(a, b, (((1,), (1,)), ((), ())), preferred_element_type=F32)


N_WIN = SEQ // WINDOW
CTX_QBLOCKS = CTX_LEN // WINDOW


def _win_attn_kernel(q_ref, kp_ref, kc_ref, kn_ref, vp_ref, vc_ref, vn_ref, kx_ref, vx_ref,
                     sink_ref, o_ref):
    n = pl.program_id(1)
    w = WINDOW
    k_all = jnp.concatenate([kp_ref[...], kc_ref[...], kn_ref[...], kx_ref[...]], axis=0)
    v_all = jnp.concatenate([vp_ref[...], vc_ref[...], vn_ref[...], vx_ref[...]], axis=0)
    nk = 3 * w + CTX_LEN
    qi = lax.broadcasted_iota(jnp.int32, (w, nk), 0)
    ki = lax.broadcasted_iota(jnp.int32, (w, nk), 1)
    rel = ki - w - qi
    kpos = (n - 1) * w + ki
    kpos_end = jnp.where(n < N_WIN, SEQ, -1)
    valid = (ki >= 3 * w) | ((jnp.abs(rel) <= w) & (kpos >= 0) & (kpos < kpos_end))
    for h in range(A_Q_HEADS):
        hk = h // (A_Q_HEADS // A_KV_HEADS)
        qh = q_ref[:, h * HEAD_DIM:(h + 1) * HEAD_DIM]
        kh = k_all[:, hk * HEAD_DIM:(hk + 1) * HEAD_DIM]
        vh = v_all[:, hk * HEAD_DIM:(hk + 1) * HEAD_DIM]
        s = jnp.where(valid, _nt_dot(qh, kh), NEG_INF)
        o = _softmax_pv(s, sink_ref[h:h + 1, 0:1], vh)
        o_ref[:, h * HEAD_DIM:(h + 1) * HEAD_DIM] = o.astype(o_ref.dtype)


def _win_attn(q, k, v, sink_b):
    def qblk(b, n):
        return (jnp.where(n < N_WIN, b * N_WIN + n, BATCH * N_WIN + b * CTX_QBLOCKS + (n - N_WIN)), 0)

    def local(off):
        return lambda b, n: (b * N_WIN + jnp.clip(n + off, 0, N_WIN - 1), 0)

    kv_spec = lambda f: pl.BlockSpec((WINDOW, DKV), f)
    ctx_spec = pl.BlockSpec((CTX_LEN, DKV), lambda b, n: (LAT_TILES + b, 0))
    return pl.pallas_call(
        _win_attn_kernel,
        grid=(BATCH, N_WIN + CTX_QBLOCKS),
        in_specs=[pl.BlockSpec((WINDOW, DQ), qblk),
                  kv_spec(local(-1)), kv_spec(local(0)), kv_spec(local(1)),
                  kv_spec(local(-1)), kv_spec(local(0)), kv_spec(local(1)),
                  ctx_spec, ctx_spec, _full((A_Q_HEADS, LANE))],
        out_specs=pl.BlockSpec((WINDOW, DQ), qblk),
        out_shape=jax.ShapeDtypeStruct((N_TOK, DQ), BF16),
        compiler_params=_params("parallel", "parallel"),
        name="win_attn",
    )(q, k, k, k, v, v, v, k, v, sink_b)


def _conv_kernel(zp_ref, zc_ref, zn_ref, w_ref, b_ref, g_ref, bb_ref, o_ref, pad_ref):
    i = pl.program_id(0)
    tl = TOK_TILE
    h = CONV_HALO
    pos = i % TILES_PER_SEQ
    first = (i >= LAT_TILES) | (pos == 0)
    last = (i >= LAT_TILES) | (pos == TILES_PER_SEQ - 1)
    pad_ref[0:h, :] = jnp.where(first, 0.0, zp_ref[...].astype(F32))
    pad_ref[h:h + tl, :] = zc_ref[...].astype(F32)
    pad_ref[h + tl:2 * h + tl, :] = jnp.where(last, 0.0, zn_ref[...].astype(F32))
    off = h - CONV_WIDTH // 2
    acc = pad_ref[off:off + tl, :] * w_ref[0:1, :]
    for j in range(1, CONV_WIDTH):
        acc = acc + pad_ref[off + j:off + j + tl, :] * w_ref[j:j + 1, :]
    z = acc + b_ref[...]
    mu = jnp.mean(z, axis=-1, keepdims=True)
    zc = z - mu
    y = zc * lax.rsqrt(jnp.mean(zc * zc, axis=-1, keepdims=True) + EPS)
    y = y * g_ref[...] + bb_ref[...]
    o_ref[...] = (y * jax.nn.sigmoid(y)).astype(o_ref.dtype)


def _conv_module(z, dw_w, dw_b, ln_g, ln_b):
    r = TOK_TILE // CONV_HALO
    n_halo = N_TOK // CONV_HALO
    halo = lambda f: pl.BlockSpec((CONV_HALO, CONV_CH), f)
    vec = _full((1, CONV_CH))
    return pl.pallas_call(
        _conv_kernel,
        grid=(N_TOK // TOK_TILE,),
        in_specs=[halo(lambda i: (jnp.maximum(i * r - 1, 0), 0)),
                  pl.BlockSpec((TOK_TILE, CONV_CH), lambda i: (i, 0)),
                  halo(lambda i: (jnp.minimum((i + 1) * r, n_halo - 1), 0)),
                  _full((CONV_WIDTH, CONV_CH)), vec, vec, vec],
        out_specs=pl.BlockSpec((TOK_TILE, CONV_CH), lambda i: (i, 0)),
        out_shape=jax.ShapeDtypeStruct((N_TOK, CONV_CH), BF16),
        scratch_shapes=[pltpu.VMEM((TOK_TILE + 2 * CONV_HALO, CONV_CH), F32)],
        compiler_params=_params("parallel"),
        name="conv_module",
    )(z, z, z, dw_w, dw_b.reshape(1, -1), ln_g.reshape(1, -1), ln_b.reshape(1, -1))


def _mla_proj_kernel(x_ref, g_ref, sh_ref, sc_ref, win_ref, qag_ref, kvag_ref, krg_ref, wq_ref, wkv_ref,
                     qg_ref, kg_ref, cos_ref, sin_ref, q_ref, k_ref, v_ref):
    h = _modulated_norm(x_ref, g_ref, sh_ref, sc_ref)
    p = jnp.dot(h.astype(BF16), win_ref[...], preferred_element_type=F32)
    lane = lax.broadcasted_iota(jnp.int32, (TOK_TILE, LANE), 1)
    half0 = (lane & (MLA_ROPE // 2 - 1)) < MLA_ROPE // 4
    cos = cos_ref[...]
    sin = sin_ref[...]
    rope = lambda t: _rope_lanes(t, cos, sin, half0, MLA_ROPE // 4)

    cq = _rms(p[:, :Q_LORA]) * qag_ref[...]
    ckv = _rms(p[:, Q_LORA:Q_LORA + KV_LORA]) * kvag_ref[...]
    kr = rope(_rms(p[:, Q_LORA + KV_LORA:], MLA_ROPE) * krg_ref[...])

    q = jnp.dot(cq.astype(BF16), wq_ref[...], preferred_element_type=F32)
    scale = MLA_QK ** -0.5
    for hd in range(MLA_HEADS):
        qh = _rms(q[:, hd * MLA_PAD:(hd + 1) * MLA_PAD], MLA_QK) * qg_ref[...]
        q_ref[:, hd * MLA_PAD:(hd + 1) * MLA_PAD] = (rope(qh) * scale).astype(BF16)

    kv = jnp.dot(ckv.astype(BF16), wkv_ref[...], preferred_element_type=F32)
    for hd in range(MLA_HEADS):
        kh = _rms(kv[:, hd * MLA_PAD:(hd + 1) * MLA_PAD], MLA_NOPE) * kg_ref[...]
        k_ref[:, hd * MLA_PAD:(hd + 1) * MLA_PAD] = (kh + kr).astype(BF16)
    v_ref[...] = kv[:, MLA_HEADS * MLA_PAD:].astype(BF16)


def _mla_proj(xt, g, mods4, w_in, qa_g, kva_g, kr_g, w_q, w_kv, q_g, k_g, cos_t, sin_t):
    tab = pl.BlockSpec((TOK_TILE, LANE), lambda i: (_rope_block_of_tile(i), 0))

    def key_blk(i):
        return (jnp.where(i < LAT_TILES, i // TILES_PER_SEQ, i - LAT_TILES), _rope_block_of_tile(i), 0)

    kv_n = MLA_HEADS * (MLA_PAD + MLA_V)
    return pl.pallas_call(
        _mla_proj_kernel,
        grid=(N_TOK // TOK_TILE,),
        in_specs=[pl.BlockSpec((TOK_TILE, D_MODEL), lambda i: (i, 0)), _full((1, D_MODEL)),
                  _mod_spec(0, TOK_TILE), _mod_spec(1, TOK_TILE),
                  _full((D_MODEL, C_IN_PAD)), _full((1, Q_LORA)), _full((1, KV_LORA)), _full((1, LANE)),
                  _full((Q_LORA, MLA_HEADS * MLA_PAD)), _full((KV_LORA, kv_n)),
                  _full((1, LANE)), _full((1, LANE)), tab, tab],
        out_specs=[pl.BlockSpec((TOK_TILE, MLA_HEADS * MLA_PAD), lambda i: (i, 0)),
                   pl.BlockSpec((None, TOK_TILE, MLA_HEADS * MLA_PAD), key_blk),
                   pl.BlockSpec((None, TOK_TILE, MLA_HEADS * MLA_V), key_blk)],
        out_shape=[jax.ShapeDtypeStruct((N_TOK, MLA_HEADS * MLA_PAD), BF16),
                   jax.ShapeDtypeStruct((BATCH, MLA_KEYS, MLA_HEADS * MLA_PAD), BF16),
                   jax.ShapeDtypeStruct((BATCH, MLA_KEYS, MLA_HEADS * MLA_V), BF16)],
        compiler_params=_params("parallel"),
        name="mla_proj",
    )(xt, g.reshape(1, D_MODEL), mods4, mods4, w_in, qa_g, kva_g, kr_g, w_q, w_kv, q_g, k_g, cos_t, sin_t)


def _mla_kernel(q_ref, k_ref, v_ref, *rest):
    o_ref = rest[-1]
    outs = []
    for j in range(2):
        qh = q_ref[:, j * MLA_PAD:(j + 1) * MLA_PAD]
        kh = k_ref[:, j * MLA_PAD:(j + 1) * MLA_PAD]
        s = _nt_dot(qh, kh)
        m = jnp.max(s, axis=-1, keepdims=True)
        p = jnp.exp(s - m)
        l = jnp.sum(p, axis=-1, keepdims=True)
        outs.append(jnp.dot(p.astype(BF16), v_ref[...], preferred_element_type=F32) / l)
    lane = lax.broadcasted_iota(jnp.int32, outs[0].shape, 1)
    o_ref[...] = jnp.where(lane < MLA_V, outs[0], outs[1]).astype(o_ref.dtype)


def _mla_attn_latent(q, k, v):
    qblk = lambda b, p, i: (b * TILES_PER_SEQ + i, p)
    return pl.pallas_call(
        _mla_kernel,
        grid=(BATCH, MLA_HEADS // 2, TILES_PER_SEQ),
        in_specs=[pl.BlockSpec((TOK_TILE, 2 * MLA_PAD), qblk),
                  pl.BlockSpec((None, MLA_KEYS, 2 * MLA_PAD), lambda b, p, i: (b, 0, p)),
                  pl.BlockSpec((None, MLA_KEYS, 2 * MLA_V), lambda b, p, i: (b, 0, p))],
        out_specs=pl.BlockSpec((TOK_TILE, 2 * MLA_V), qblk),
        out_shape=jax.ShapeDtypeStruct((N_TOK, MLA_HEADS * MLA_V), BF16),
        compiler_params=_params("parallel", "parallel", "parallel"),
        name="mla_attn",
    )(q, k, v)


def _mla_attn_ctx(q, k, v, o):
    qblk = lambda b, p: (LAT_TILES + b, p)
    tail = lambda b, p: (b, TILES_PER_SEQ, p)
    return pl.pallas_call(
        _mla_kernel,
        grid=(BATCH, MLA_HEADS // 2),
        in_specs=[pl.BlockSpec((TOK_TILE, 2 * MLA_PAD), qblk),
                  pl.BlockSpec((None, CTX_LEN, 2 * MLA_PAD), tail),
                  pl.BlockSpec((None, CTX_LEN, 2 * MLA_V), tail),
                  pl.BlockSpec(memory_space=pl.ANY)],
        out_specs=pl.BlockSpec((TOK_TILE, 2 * MLA_V), qblk),
        out_shape=jax.ShapeDtypeStruct((N_TOK, MLA_HEADS * MLA_V), BF16),
        input_output_aliases={3: 0},
        compiler_params=_params("parallel", "parallel"),
        name="mla_attn_ctx",
    )(q, k, v, o)


def _outproj_kernel(n_in, *refs):
    a_refs = refs[:n_in]
    w_refs = refs[n_in:2 * n_in]
    x_ref, g1_ref, gn_ref, sh_ref, sc_ref, rw_ref, rb_ref, xo_ref, f_ref, lg_ref = refs[2 * n_in:]
    y = jnp.dot(a_refs[0][...], w_refs[0][...], preferred_element_type=F32)
    for a_ref, w_ref in zip(a_refs[1:], w_refs[1:]):
        y = y + jnp.dot(a_ref[...], w_ref[...], preferred_element_type=F32)
    xo_ref[...] = x_ref[...] + g1_ref[...] * y
    f = _modulated_norm(xo_ref, gn_ref, sh_ref, sc_ref)
    f_ref[...] = f
    lg_ref[...] = jnp.dot(f, rw_ref[...], precision=lax.Precision.HIGHEST,
                          preferred_element_type=F32) + rb_ref[...]


def _outproj(a_list, w_list, x, n_rows, mods4, gn, rw_pad, rb_pad):
    n_in = len(a_list)
    row = lambda n: pl.BlockSpec((OUT_TILE, n), lambda i: (i, 0))
    in_specs = [row(a.shape[1]) for a in a_list] + [_full(w.shape) for w in w_list]
    in_specs += [row(D_MODEL), _mod_spec(2, OUT_TILE), _full((1, D_MODEL)),
                 _mod_spec(3, OUT_TILE), _mod_spec(4, OUT_TILE), _full((D_MODEL, LANE)), _full((1, LANE))]
    return pl.pallas_call(
        functools.partial(_outproj_kernel, n_in),
        grid=(n_rows // OUT_TILE,),
        in_specs=in_specs,
        out_specs=[row(D_MODEL), row(D_MODEL), row(LANE)],
        out_shape=[jax.ShapeDtypeStruct((n_rows, D_MODEL), F32),
                   jax.ShapeDtypeStruct((n_rows, D_MODEL), F32),
                   jax.ShapeDtypeStruct((n_rows, LANE), F32)],
        compiler_params=_params("parallel"),
        name="outproj",
    )(*a_list, *w_list, x, mods4, gn.reshape(1, D_MODEL), mods4, mods4, rw_pad, rb_pad)


IDX_LANE = 0
GATE_LANE = TOP_K
POS_LANE = 2 * TOP_K
MASKED = -3.0e38


def _route_kernel(lg_ref, info_ref, cnt_ref, carry_ref):
    i = pl.program_id(0)

    @pl.when(i == 0)
    def _():
        carry_ref[...] = jnp.zeros_like(carry_ref)

    lane = lax.broadcasted_iota(jnp.int32, (TOK_TILE, LANE), 1).astype(F32)
    x = jnp.where(lane < N_EXPERTS, lg_ref[...], MASKED)
    vals, idxs = [], []
    for _ in range(TOP_K):
        m = jnp.max(x, axis=-1, keepdims=True)
        idx = jnp.min(jnp.where(x == m, lane, float(LANE)), axis=-1, keepdims=True)
        vals.append(m)
        idxs.append(idx)
        x = jnp.where(lane == idx, MASKED, x)
    es = [jnp.exp(v - vals[0]) for v in vals]
    denom = es[0] + es[1] + es[2] + es[3]
    onehot = jnp.zeros((TOK_TILE, LANE), F32)
    for idx in idxs:
        onehot = onehot + jnp.where(lane == idx, 1.0, 0.0)
    r = lax.broadcasted_iota(jnp.int32, (TOK_TILE, TOK_TILE), 0)
    c = lax.broadcasted_iota(jnp.int32, (TOK_TILE, TOK_TILE), 1)
    tri = jnp.where(r > c, 1.0, 0.0).astype(BF16)
    before = jnp.dot(tri, onehot.astype(BF16), preferred_element_type=F32) + carry_ref[...]
    info = jnp.zeros((TOK_TILE, LANE), F32)
    for k in range(TOP_K):
        pos = jnp.sum(jnp.where(lane == idxs[k], before, 0.0), axis=-1, keepdims=True)
        info = jnp.where(lane == IDX_LANE + k, idxs[k], info)
        info = jnp.where(lane == GATE_LANE + k, es[k] / denom, info)
        info = jnp.where(lane == POS_LANE + k, pos, info)
    info_ref[...] = info
    carry_ref[...] = carry_ref[...] + jnp.sum(onehot, axis=0, keepdims=True)
    cnt_ref[...] = carry_ref[...]


def _route(logits, n_rows):
    return pl.pallas_call(
        _route_kernel,
        grid=(n_rows // TOK_TILE,),
        in_specs=[pl.BlockSpec((TOK_TILE, LANE), lambda i: (i, 0))],
        out_specs=[pl.BlockSpec((TOK_TILE, LANE), lambda i: (i, 0)), _full((1, LANE))],
        out_shape=[jax.ShapeDtypeStruct((n_rows, LANE), F32), jax.ShapeDtypeStruct((1, LANE), F32)],
        scratch_shapes=[pltpu.VMEM((1, LANE), F32)],
        compiler_params=_params("arbitrary"),
        name="route",
    )(logits)


def _row_copy(src_ref, src_row, dst_ref, dst_row, sem):
    return pltpu.make_async_copy(src_ref.at[pl.ds(src_row, 1)], dst_ref.at[pl.ds(dst_row, 1)], sem)


def _dispatch_kernel(dest_ref, f_ref, xb_ref, sem):
    def issue(r, carry):
        for k in range(TOP_K):
            _row_copy(f_ref, r, xb_ref, dest_ref[0, r * TOP_K + k], sem).start()
        return carry

    lax.fori_loop(0, TOK_TILE, issue, 0, unroll=8)
    for _ in range(TOP_K):
        pltpu.make_async_copy(f_ref, xb_ref.at[pl.ds(0, TOK_TILE)], sem).wait()


def _dispatch(dest3, f, n_rows_out):
    n_rows = dest3.shape[0] * TOK_TILE
    return pl.pallas_call(
        _dispatch_kernel,
        grid=(n_rows // TOK_TILE,),
        in_specs=[pl.BlockSpec((None, 1, TOK_TILE * TOP_K), lambda i: (i, 0, 0), memory_space=pltpu.SMEM),
                  pl.BlockSpec((TOK_TILE, D_MODEL), lambda i: (i, 0))],
        out_specs=pl.BlockSpec(memory_space=pl.ANY),
        out_shape=jax.ShapeDtypeStruct((n_rows_out, D_MODEL), F32),
        scratch_shapes=[pltpu.SemaphoreType.DMA(())],
        compiler_params=_params("arbitrary"),
        name="moe_dispatch",
    )(dest3, f)


def _moe_kernel(be_ref, nv_ref, x_ref, wgu_ref, bgu_ref, wd_ref, bd_ref, o_ref, wgu_s, wd_s):
    i = pl.program_id(0)
    e = be_ref[i]
    e_prev = be_ref[jnp.maximum(i - 1, 0)]

    @pl.when((i == 0) | (e != e_prev))
    def _():
        wgu_s[...] = wgu_ref[...].astype(BF16)
        wd_s[...] = wd_ref[...].astype(BF16)

    @pl.when(nv_ref[i] > 0)
    def _():
        row = lax.broadcasted_iota(jnp.int32, (MOE_ROWS, D_MODEL), 0)
        x = jnp.where(row < nv_ref[i], x_ref[...], 0.0).astype(BF16)
        gu = jnp.dot(x, wgu_s[...], preferred_element_type=F32) + bgu_ref[...]
        gate = jnp.minimum(gu[:, :D_EXPERT], SWIGLU_LIMIT)
        up = jnp.clip(gu[:, D_EXPERT:], -SWIGLU_LIMIT, SWIGLU_LIMIT)
        act = (up + 1.0) * (gate * jax.nn.sigmoid(SWIGLU_ALPHA * gate))
        o_ref[...] = jnp.dot(act.astype(BF16), wd_s[...], preferred_element_type=F32) + bd_ref[...]


def _moe_experts(block_expert, rows_valid, xb, w_gu, b_gu, w_down, b_down):
    n_rows = xb.shape[0]
    grid_spec = pltpu.PrefetchScalarGridSpec(
        num_scalar_prefetch=2,
        grid=(n_rows // MOE_ROWS,),
        in_specs=[
            pl.BlockSpec((MOE_ROWS, D_MODEL), lambda i, be, nv: (i, 0)),
            pl.BlockSpec((None, D_MODEL, 2 * D_EXPERT), lambda i, be, nv: (be[i], 0, 0)),
            pl.BlockSpec((None, 1, 2 * D_EXPERT), lambda i, be, nv: (be[i], 0, 0)),
            pl.BlockSpec((None, D_EXPERT, D_MODEL), lambda i, be, nv: (be[i], 0, 0)),
            pl.BlockSpec((None, 1, D_MODEL), lambda i, be, nv: (be[i], 0, 0)),
        ],
        out_specs=pl.BlockSpec((MOE_ROWS, D_MODEL), lambda i, be, nv: (i, 0)),
        scratch_shapes=[pltpu.VMEM((D_MODEL, 2 * D_EXPERT), BF16),
                        pltpu.VMEM((D_EXPERT, D_MODEL), BF16)],
    )
    return pl.pallas_call(
        _moe_kernel,
        grid_spec=grid_spec,
        out_shape=jax.ShapeDtypeStruct((n_rows, D_MODEL), F32),
        compiler_params=_params("arbitrary"),
        name="moe_experts",
    )(block_expert, rows_valid, xb, w_gu, b_gu.reshape(N_EXPERTS, 1, -1), w_down,
      b_down.reshape(N_EXPERTS, 1, -1))


def _combine_kernel(dest_ref, x_ref, g2_ref, info_ref, yb_ref, o_ref, buf, sem):
    def issue(r, carry):
        for k in range(TOP_K):
            _row_copy(yb_ref, dest_ref[0, r * TOP_K + k], buf.at[k], r, sem).start()
        return carry

    lax.fori_loop(0, TOK_TILE, issue, 0, unroll=8)
    for k in range(TOP_K):
        pltpu.make_async_copy(yb_ref.at[pl.ds(0, TOK_TILE)], buf.at[k], sem).wait()
    info = info_ref[...]
    y = info[:, GATE_LANE:GATE_LANE + 1] * buf[0]
    for k in range(1, TOP_K):
        y = y + info[:, GATE_LANE + k:GATE_LANE + k + 1] * buf[k]
    o_ref[...] = x_ref[...] + g2_ref[...] * y


def _combine(dest3, x, mods4, info, yb):
    n_rows = dest3.shape[0] * TOK_TILE
    row = lambda n: pl.BlockSpec((TOK_TILE, n), lambda i: (i, 0))
    return pl.pallas_call(
        _combine_kernel,
        grid=(n_rows // TOK_TILE,),
        in_specs=[pl.BlockSpec((None, 1, TOK_TILE * TOP_K), lambda i: (i, 0, 0), memory_space=pltpu.SMEM),
                  row(D_MODEL), _mod_spec(5, TOK_TILE), row(LANE), pl.BlockSpec(memory_space=pl.ANY)],
        out_specs=row(D_MODEL),
        out_shape=jax.ShapeDtypeStruct((n_rows, D_MODEL), F32),
        scratch_shapes=[pltpu.VMEM((TOP_K, TOK_TILE, D_MODEL), F32), pltpu.SemaphoreType.DMA(())],
        compiler_params=_params("arbitrary"),
        name="moe_combine",
    )(dest3, x, mods4, info, yb)


def _moe_layer(x_new, f, logits, n_rows, mods4, w_gu, b_gu, w_down, b_down):
    info, cnt = _route(logits, n_rows)
    idx = info[:, IDX_LANE:IDX_LANE + TOP_K].astype(jnp.int32)
    pos = info[:, POS_LANE:POS_LANE + TOP_K].astype(jnp.int32)
    counts = cnt[0, :N_EXPERTS].astype(jnp.int32)
    padded = (counts + MOE_ROWS - 1) // MOE_ROWS * MOE_ROWS
    pad_ends = jnp.cumsum(padded)
    pad_starts = pad_ends - padded
    experts = jnp.arange(N_EXPERTS, dtype=jnp.int32)
    dest = pos + jnp.sum(jnp.where(idx[:, :, None] == experts, pad_starts, 0), axis=-1)
    dest3 = dest.reshape(n_rows // TOK_TILE, 1, TOK_TILE * TOP_K)
    n_blocks = n_rows * TOP_K // MOE_ROWS + N_EXPERTS
    block_start = jnp.arange(n_blocks, dtype=jnp.int32) * MOE_ROWS
    block_expert = jnp.minimum(jnp.sum((block_start[:, None] >= pad_ends[None, :]).astype(jnp.int32), axis=1),
                               N_EXPERTS - 1)
    group_end = (pad_starts + counts)[block_expert]
    rows_valid = jnp.clip(group_end - block_start, 0, MOE_ROWS).astype(jnp.int32)
    xb = _dispatch(dest3, f, n_blocks * MOE_ROWS)
    yb = _moe_experts(block_expert, rows_valid, xb, w_gu, b_gu, w_down, b_down)
    return _combine(dest3, x_new, mods4, info, yb)


def _pad_lanes(v, start=0):
    return jnp.zeros((1, LANE), F32).at[0, start:start + v.shape[0]].set(v)


def kernel(x, c, ctx, c_ctx, mod_w, mod_b, norm_mix_g, norm_ffn_g, ab_w_in, ab_w_out, a_q_norm, a_k_norm, a_sink, b_dw_w, b_dw_b, b_ln_g, b_ln_b, c_w_in, c_q_a_norm, c_kv_a_norm, c_w_q_b, c_w_kv_b, c_q_norm, c_k_norm, c_kr_norm, c_w_out, router_w, router_b, exp_w_gu, exp_b_gu, exp_w_down, exp_b_down):
    cos_a, sin_a = _rope_lane_tables(HEAD_DIM, 0, LANE // HEAD_DIM)
    cos_c, sin_c = _rope_lane_tables(MLA_ROPE, MLA_NOPE, 1)
    cvec = jnp.concatenate([c, c_ctx[None, :], jnp.zeros((SEG_PAD - N_SEG, D_MODEL), F32)], axis=0)
    mods = _modulation(cvec, mod_w, mod_b)
    xt = jnp.concatenate([x.reshape(N_LAT, D_MODEL), ctx.reshape(N_CTX, D_MODEL)], axis=0)
    for layer in range(DEPTH):
        keep_ctx = layer < DEPTH - 1
        n_rows = N_TOK if keep_ctx else N_LAT
        mods4 = mods[layer].reshape(SEG_PAD, 6, 1, D_MODEL)
        i = layer // 2
        if layer % 2 == 0:
            qg2 = jnp.tile(a_q_norm[i], LANE // HEAD_DIM).reshape(1, LANE)
            kg2 = jnp.tile(a_k_norm[i], LANE // HEAD_DIM).reshape(1, LANE)
            q, k, v, z = _even_inproj(xt, norm_mix_g[layer], mods4, ab_w_in[i].astype(BF16), qg2, kg2, cos_a, sin_a)
            sink_b = jnp.broadcast_to(a_sink[i].astype(F32)[:, None], (A_Q_HEADS, LANE))
            a = _win_attn(q, k, v, sink_b)
            b = _conv_module(z, b_dw_w[i], b_dw_b[i], b_ln_g[i], b_ln_b[i])
            w_out = ab_w_out[i].astype(BF16)
            a_list, w_list = [a, b], [w_out[:DQ], w_out[DQ:]]
        else:
            w_in = c_w_in[i]
            zeros = lambda n: jnp.zeros((D_MODEL, n), F32)
            w_in_p = jnp.concatenate([w_in[:, :Q_LORA + KV_LORA], zeros(MLA_NOPE), w_in[:, Q_LORA + KV_LORA:],
                                      zeros(LANE - MLA_QK)], axis=1).astype(BF16)
            w_q = jnp.pad(c_w_q_b[i].reshape(Q_LORA, MLA_HEADS, MLA_QK),
                          ((0, 0), (0, 0), (0, MLA_PAD - MLA_QK))).reshape(Q_LORA, MLA_HEADS * MLA_PAD)
            w_kv = c_w_kv_b[i].reshape(KV_LORA, MLA_HEADS, MLA_NOPE + MLA_V)
            w_k = jnp.pad(w_kv[:, :, :MLA_NOPE], ((0, 0), (0, 0), (0, MLA_PAD - MLA_NOPE)))
            w_kv_p = jnp.concatenate([w_k.reshape(KV_LORA, -1), w_kv[:, :, MLA_NOPE:].reshape(KV_LORA, -1)], axis=1)
            q, k, v = _mla_proj(xt, norm_mix_g[layer], mods4, w_in_p, c_q_a_norm[i].reshape(1, -1),
                                c_kv_a_norm[i].reshape(1, -1), _pad_lanes(c_kr_norm[i], MLA_NOPE),
                                w_q.astype(BF16), w_kv_p.astype(BF16), _pad_lanes(c_q_norm[i]),
                                _pad_lanes(c_k_norm[i]), cos_c, sin_c)
            o = _mla_attn_latent(q, k, v)
            if keep_ctx:
                o = _mla_attn_ctx(q, k, v, o)
            a_list, w_list = [o], [c_w_out[i].astype(BF16)]
        rw_pad = jnp.pad(router_w[layer], ((0, 0), (0, LANE - N_EXPERTS)))
        rb_pad = jnp.pad(router_b[layer], (0, LANE - N_EXPERTS)).reshape(1, LANE)
        x_new, f, logits = _outproj(a_list, w_list, xt, n_rows, mods4, norm_ffn_g[layer], rw_pad, rb_pad)
        xt = _moe_layer(x_new, f, logits, n_rows, mods4, exp_w_gu[layer], exp_b_gu[layer],
                        exp_w_down[layer], exp_b_down[layer])
    return xt.reshape(BATCH, SEQ, D_MODEL)
```

```python
import functools

import jax
import jax.numpy as jnp
import numpy as np
from jax import lax
from jax.experimental import pallas as pl
from jax.experimental.pallas import tpu as pltpu

D_MODEL = 1024
BATCH = 4
SEQ = 4096
DEPTH = 4
GRID_W = 64
CTX_LEN = 256
A_Q_HEADS = 8
A_KV_HEADS = 2
HEAD_DIM = 64
WINDOW = 128
CONV_CH = 512
CONV_WIDTH = 31
MLA_HEADS = 16
MLA_NOPE = 64
MLA_ROPE = 32
MLA_V = 64
MLA_QK = MLA_NOPE + MLA_ROPE
Q_LORA = 768
KV_LORA = 256
N_EXPERTS = 32
TOP_K = 4
D_EXPERT = 1024
SWIGLU_LIMIT = 7.0
SWIGLU_ALPHA = 1.702
ROPE_THETA = 10000.0
EPS = 1e-6
NEG_INF = -1e30

LANE = 128
N_LAT = BATCH * SEQ
N_CTX = BATCH * CTX_LEN
N_TOK = N_LAT + N_CTX
N_SEG = BATCH + 1
SEG_PAD = 8
DQ = A_Q_HEADS * HEAD_DIM
DKV = A_KV_HEADS * HEAD_DIM
DKV_DUP = A_KV_HEADS * LANE
AB_IN = DQ + 2 * DKV + 2 * CONV_CH
MLA_PAD = LANE
C_IN_PAD = Q_LORA + KV_LORA + LANE
MLA_KEYS = SEQ + CTX_LEN

TOK_TILE = 256
LAT_TILES = N_LAT // TOK_TILE
TILES_PER_SEQ = SEQ // TOK_TILE
OUT_TILE = 512
MOE_ROWS = 256
CONV_HALO = 16
VMEM_LIMIT = 56 * 1024 * 1024

F32 = jnp.float32
BF16 = jnp.bfloat16


def _params(*sem):
    return pltpu.CompilerParams(dimension_semantics=sem, vmem_limit_bytes=VMEM_LIMIT)


def _seg_of_tile(i, tile):
    return jnp.minimum(i * tile // SEQ, BATCH)


def _rope_block_of_tile(i):
    return jnp.where(i < LAT_TILES, i % TILES_PER_SEQ, TILES_PER_SEQ)


def _mod_spec(chunk, tile):
    return pl.BlockSpec((None, None, 1, D_MODEL), lambda i: (_seg_of_tile(i, tile), chunk, 0, 0))


def _full(shape):
    return pl.BlockSpec(shape, lambda *_: (0,) * len(shape))


def _rms(x, n=None):
    n = x.shape[-1] if n is None else n
    return x * lax.rsqrt(jnp.sum(x * x, axis=-1, keepdims=True) * (1.0 / n) + EPS)


def _modulated_norm(x_ref, g_ref, sh_ref, sc_ref):
    h = _rms(x_ref[...]) * g_ref[...]
    return h * (1.0 + sc_ref[...]) + sh_ref[...]


def _mod_kernel(c_ref, w_ref, b_ref, o_ref):
    c = c_ref[...]
    s = c * jax.nn.sigmoid(c)
    o_ref[...] = jnp.dot(s, w_ref[...], precision=lax.Precision.HIGHEST,
                         preferred_element_type=F32) + b_ref[...]


def _modulation(cvec, mod_w, mod_b):
    tn = 1536
    return pl.pallas_call(
        _mod_kernel,
        grid=(DEPTH, 6 * D_MODEL // tn),
        in_specs=[
            pl.BlockSpec((SEG_PAD, D_MODEL), lambda l, j: (0, 0)),
            pl.BlockSpec((None, D_MODEL, tn), lambda l, j: (l, 0, j)),
            pl.BlockSpec((None, 1, tn), lambda l, j: (l, 0, j)),
        ],
        out_specs=pl.BlockSpec((None, SEG_PAD, tn), lambda l, j: (l, 0, j)),
        out_shape=jax.ShapeDtypeStruct((DEPTH, SEG_PAD, 6 * D_MODEL), F32),
        compiler_params=_params("arbitrary", "arbitrary"),
        name="modulation",
    )(cvec, mod_w, mod_b.reshape(DEPTH, 1, 6 * D_MODEL))


def _rope_lanes(t, cos, sin, half0, shift):
    partner = jnp.where(half0, pltpu.roll(t, LANE - shift, 1), pltpu.roll(t, shift, 1))
    return t * cos + partner * sin


def _rope_lane_tables(rot_dims, lane_start, repeat):
    rows = SEQ // GRID_W
    row = jnp.broadcast_to(jnp.arange(rows)[:, None], (rows, GRID_W)).reshape(-1).astype(F32)
    col = jnp.broadcast_to(jnp.arange(GRID_W)[None, :], (rows, GRID_W)).reshape(-1).astype(F32)
    axis_dims = rot_dims // 2
    n = axis_dims // 2
    inv = ROPE_THETA ** (-(jnp.arange(n, dtype=F32) * 2.0) / axis_dims)
    ang = jnp.concatenate([row[:, None] * inv, col[:, None] * inv], axis=-1)
    d = np.arange(rot_dims)
    src = (d // (2 * n)) * n + d % n
    sign = np.where((d % (2 * n)) // n == 0, -1.0, 1.0).astype(np.float32)
    cos_blk = jnp.cos(ang)[:, src]
    sin_blk = jnp.sin(ang)[:, src] * sign
    cos_t = jnp.ones((SEQ + TOK_TILE, LANE), F32)
    sin_t = jnp.zeros((SEQ + TOK_TILE, LANE), F32)
    for r in range(repeat):
        lo = lane_start + r * rot_dims
        cos_t = cos_t.at[:SEQ, lo:lo + rot_dims].set(cos_blk)
        sin_t = sin_t.at[:SEQ, lo:lo + rot_dims].set(sin_blk)
    return cos_t, sin_t


def _even_inproj_kernel(x_ref, g_ref, sh_ref, sc_ref, w_ref, qg_ref, kg_ref, cos_ref, sin_ref,
                        q_ref, k_ref, v_ref, z_ref):
    h = _modulated_norm(x_ref, g_ref, sh_ref, sc_ref)
    p = jnp.dot(h.astype(BF16), w_ref[...], preferred_element_type=F32)
    lane = lax.broadcasted_iota(jnp.int32, (TOK_TILE, LANE), 1)
    lo = lane < HEAD_DIM
    half0 = (lane & (HEAD_DIM // 2 - 1)) < HEAD_DIM // 4
    cos = cos_ref[...]
    sin = sin_ref[...]

    def norm_rope(t, gvec):
        ss = t * t
        s_lo = jnp.sum(jnp.where(lo, ss, 0.0), axis=-1, keepdims=True)
        s_hi = jnp.sum(jnp.where(lo, 0.0, ss), axis=-1, keepdims=True)
        ms = jnp.where(lo, s_lo, s_hi) * (1.0 / HEAD_DIM)
        tn = t * lax.rsqrt(ms + EPS) * gvec
        return _rope_lanes(tn, cos, sin, half0, HEAD_DIM // 4)

    scale = HEAD_DIM ** -0.5
    for j in range(DQ // LANE):
        q_ref[:, j * LANE:(j + 1) * LANE] = (norm_rope(p[:, j * LANE:(j + 1) * LANE], qg_ref[...]) * scale).astype(BF16)
    def twice(t, ref):
        sw = pltpu.roll(t, HEAD_DIM, 1)
        ref[:, :LANE] = jnp.where(lo, t, sw).astype(BF16)
        ref[:, LANE:] = jnp.where(lo, sw, t).astype(BF16)

    twice(norm_rope(p[:, DQ:DQ + DKV], kg_ref[...]), k_ref)
    twice(p[:, DQ + DKV:DQ + 2 * DKV], v_ref)
    u0 = DQ + 2 * DKV
    z_ref[...] = (p[:, u0:u0 + CONV_CH] * jax.nn.sigmoid(p[:, u0 + CONV_CH:])).astype(BF16)


def _even_inproj(xt, g, mods4, w_bf, qg2, kg2, cos_t, sin_t):
    row = lambda n: pl.BlockSpec((TOK_TILE, n), lambda i: (i, 0))
    tab = pl.BlockSpec((TOK_TILE, LANE), lambda i: (_rope_block_of_tile(i), 0))
    return pl.pallas_call(
        _even_inproj_kernel,
        grid=(N_TOK // TOK_TILE,),
        in_specs=[row(D_MODEL), _full((1, D_MODEL)), _mod_spec(0, TOK_TILE), _mod_spec(1, TOK_TILE),
                  _full((D_MODEL, AB_IN)), _full((1, LANE)), _full((1, LANE)), tab, tab],
        out_specs=[row(DQ), row(DKV_DUP), row(DKV_DUP), row(CONV_CH)],
        out_shape=[jax.ShapeDtypeStruct((N_TOK, DQ), BF16),
                   jax.ShapeDtypeStruct((N_TOK, DKV_DUP), BF16),
                   jax.ShapeDtypeStruct((N_TOK, DKV_DUP), BF16),
                   jax.ShapeDtypeStruct((N_TOK, CONV_CH), BF16)],
        compiler_params=_params("parallel"),
        name="even_inproj",
    )(xt, g.reshape(1, D_MODEL), mods4, mods4, w_bf, qg2, kg2, cos_t, sin_t)


def _softmax_pv(s, sk, v):
    m = jnp.maximum(jnp.max(s, axis=-1, keepdims=True), sk)
    p = jnp.exp(s - m)
    l = jnp.sum(p, axis=-1, keepdims=True) + jnp.exp(sk - m)
    o = jnp.dot(p.astype(BF16), v, preferred_element_type=F32)
    return o / l


def _nt_dot(a, b):
    return lax.dot_general(a, b, (((1,), (1,)), ((), ())), preferred_element_type=F32)


N_WIN = SEQ // WINDOW
CTX_QBLOCKS = CTX_LEN // WINDOW


def _win_attn_kernel(q_ref, kp_ref, kc_ref, kn_ref, vp_ref, vc_ref, vn_ref, kx_ref, vx_ref,
                     sink_ref, o_ref):
    n = pl.program_id(1)
    w = WINDOW
    k_all = jnp.concatenate([kp_ref[...], kc_ref[...], kn_ref[...], kx_ref[...]], axis=0)
    v_all = jnp.concatenate([vp_ref[...], vc_ref[...], vn_ref[...], vx_ref[...]], axis=0)
    nk = 3 * w + CTX_LEN
    group = A_Q_HEADS // A_KV_HEADS
    row = lax.broadcasted_iota(jnp.int32, (group * w, nk), 0)
    qi = row & (w - 1)
    ki = lax.broadcasted_iota(jnp.int32, (group * w, nk), 1)
    rel = ki - w - qi
    kpos = (n - 1) * w + ki
    kpos_end = jnp.where(n < N_WIN, SEQ, -1)
    valid = (ki >= 3 * w) | ((jnp.abs(rel) <= w) & (kpos >= 0) & (kpos < kpos_end))
    lo = lax.broadcasted_iota(jnp.int32, (w, LANE), 1) < HEAD_DIM
    head_of_row = lax.broadcasted_iota(jnp.int32, (group * w, 1), 0) // w
    tiles_per_kv = group * HEAD_DIM // LANE
    for g in range(A_KV_HEADS):
        kk = k_all[:, g * LANE:(g + 1) * LANE]
        vv = v_all[:, g * LANE:(g + 1) * LANE]
        parts = []
        for j in range(g * tiles_per_kv, (g + 1) * tiles_per_kv):
            qt = q_ref[:, j * LANE:(j + 1) * LANE]
            zero = jnp.zeros_like(qt)
            parts += [jnp.where(lo, qt, zero), jnp.where(lo, zero, qt)]
        qs = jnp.concatenate(parts, axis=0)
        sk = sink_ref[g * group:g * group + 1, 0:1]
        for hh in range(1, group):
            sk = jnp.where(head_of_row == hh, sink_ref[g * group + hh:g * group + hh + 1, 0:1], sk)
        s = jnp.where(valid, _nt_dot(qs, kk), NEG_INF)
        o = _softmax_pv(s, sk, vv)
        for t in range(tiles_per_kv):
            j = g * tiles_per_kv + t
            o_ref[:, j * LANE:(j + 1) * LANE] = jnp.where(
                lo, o[2 * t * w:(2 * t + 1) * w], o[(2 * t + 1) * w:(2 * t + 2) * w]).astype(o_ref.dtype)


def _win_attn(q, k, v, sink_b):
    def qblk(b, n):
        return (jnp.where(n < N_WIN, b * N_WIN + n, BATCH * N_WIN + b * CTX_QBLOCKS + (n - N_WIN)), 0)

    def local(off):
        return lambda b, n: (b * N_WIN + jnp.clip(n + off, 0, N_WIN - 1), 0)

    kv_spec = lambda f: pl.BlockSpec((WINDOW, DKV_DUP), f)
    ctx_spec = pl.BlockSpec((CTX_LEN, DKV_DUP), lambda b, n: (LAT_TILES + b, 0))
    return pl.pallas_call(
        _win_attn_kernel,
        grid=(BATCH, N_WIN + CTX_QBLOCKS),
        in_specs=[pl.BlockSpec((WINDOW, DQ), qblk),
                  kv_spec(local(-1)), kv_spec(local(0)), kv_spec(local(1)),
                  kv_spec(local(-1)), kv_spec(local(0)), kv_spec(local(1)),
                  ctx_spec, ctx_spec, _full((A_Q_HEADS, LANE))],
        out_specs=pl.BlockSpec((WINDOW, DQ), qblk),
        out_shape=jax.ShapeDtypeStruct((N_TOK, DQ), BF16),
        compiler_params=_params("parallel", "parallel"),
        name="win_attn",
    )(q, k, k, k, v, v, v, k, v, sink_b)


def _conv_kernel(zp_ref, zc_ref, zn_ref, w_ref, b_ref, g_ref, bb_ref, o_ref, pad_ref):
    i = pl.program_id(0)
    tl = TOK_TILE
    h = CONV_HALO
    pos = i % TILES_PER_SEQ
    first = (i >= LAT_TILES) | (pos == 0)
    last = (i >= LAT_TILES) | (pos == TILES_PER_SEQ - 1)
    pad_ref[0:h, :] = jnp.where(first, 0.0, zp_ref[...].astype(F32))
    pad_ref[h:h + tl, :] = zc_ref[...].astype(F32)
    pad_ref[h + tl:2 * h + tl, :] = jnp.where(last, 0.0, zn_ref[...].astype(F32))
    off = h - CONV_WIDTH // 2
    acc = pad_ref[off:off + tl, :] * w_ref[0:1, :]
    for j in range(1, CONV_WIDTH):
        acc = acc + pad_ref[off + j:off + j + tl, :] * w_ref[j:j + 1, :]
    z = acc + b_ref[...]
    mu = jnp.mean(z, axis=-1, keepdims=True)
    zc = z - mu
    y = zc * lax.rsqrt(jnp.mean(zc * zc, axis=-1, keepdims=True) + EPS)
    y = y * g_ref[...] + bb_ref[...]
    o_ref[...] = (y * jax.nn.sigmoid(y)).astype(o_ref.dtype)


def _conv_module(z, dw_w, dw_b, ln_g, ln_b):
    r = TOK_TILE // CONV_HALO
    n_halo = N_TOK // CONV_HALO
    halo = lambda f: pl.BlockSpec((CONV_HALO, CONV_CH), f)
    vec = _full((1, CONV_CH))
    return pl.pallas_call(
        _conv_kernel,
        grid=(N_TOK // TOK_TILE,),
        in_specs=[halo(lambda i: (jnp.maximum(i * r - 1, 0), 0)),
                  pl.BlockSpec((TOK_TILE, CONV_CH), lambda i: (i, 0)),
                  halo(lambda i: (jnp.minimum((i + 1) * r, n_halo - 1), 0)),
                  _full((CONV_WIDTH, CONV_CH)), vec, vec, vec],
        out_specs=pl.BlockSpec((TOK_TILE, CONV_CH), lambda i: (i, 0)),
        out_shape=jax.ShapeDtypeStruct((N_TOK, CONV_CH), BF16),
        scratch_shapes=[pltpu.VMEM((TOK_TILE + 2 * CONV_HALO, CONV_CH), F32)],
        compiler_params=_params("parallel"),
        name="conv_module",
    )(z, z, z, dw_w, dw_b.reshape(1, -1), ln_g.reshape(1, -1), ln_b.reshape(1, -1))


def _mla_proj_kernel(x_ref, g_ref, sh_ref, sc_ref, win_ref, qag_ref, kvag_ref, krg_ref, wq_ref, wkv_ref,
                     qg_ref, kg_ref, cos_ref, sin_ref, q_ref, k_ref, v_ref):
    h = _modulated_norm(x_ref, g_ref, sh_ref, sc_ref)
    p = jnp.dot(h.astype(BF16), win_ref[...], preferred_element_type=F32)
    lane = lax.broadcasted_iota(jnp.int32, (TOK_TILE, LANE), 1)
    half0 = (lane & (MLA_ROPE // 2 - 1)) < MLA_ROPE // 4
    cos = cos_ref[...]
    sin = sin_ref[...]
    rope = lambda t: _rope_lanes(t, cos, sin, half0, MLA_ROPE // 4)

    cq = _rms(p[:, :Q_LORA]) * qag_ref[...]
    ckv = _rms(p[:, Q_LORA:Q_LORA + KV_LORA]) * kvag_ref[...]
    kr = rope(_rms(p[:, Q_LORA + KV_LORA:], MLA_ROPE) * krg_ref[...])

    q = jnp.dot(cq.astype(BF16), wq_ref[...], preferred_element_type=F32)
    scale = float(MLA_QK ** -0.5 * np.log2(np.e))
    for hd in range(MLA_HEADS):
        qh = _rms(q[:, hd * MLA_PAD:(hd + 1) * MLA_PAD], MLA_QK) * qg_ref[...]
        q_ref[:, hd * MLA_PAD:(hd + 1) * MLA_PAD] = (rope(qh) * scale).astype(BF16)

    kv = jnp.dot(ckv.astype(BF16), wkv_ref[...], preferred_element_type=F32)
    for hd in range(MLA_HEADS):
        kh = _rms(kv[:, hd * MLA_PAD:(hd + 1) * MLA_PAD], MLA_NOPE) * kg_ref[...]
        k_ref[:, hd * MLA_PAD:(hd + 1) * MLA_PAD] = (kh + kr).astype(BF16)
    v_ref[...] = kv[:, MLA_HEADS * MLA_PAD:].astype(BF16)


def _mla_proj(xt, g, mods4, w_in, qa_g, kva_g, kr_g, w_q, w_kv, q_g, k_g, cos_t, sin_t):
    tab = pl.BlockSpec((TOK_TILE, LANE), lambda i: (_rope_block_of_tile(i), 0))

    def key_blk(i):
        return (jnp.where(i < LAT_TILES, i // TILES_PER_SEQ, i - LAT_TILES), _rope_block_of_tile(i), 0)

    kv_n = MLA_HEADS * (MLA_PAD + MLA_V)
    return pl.pallas_call(
        _mla_proj_kernel,
        grid=(N_TOK // TOK_TILE,),
        in_specs=[pl.BlockSpec((TOK_TILE, D_MODEL), lambda i: (i, 0)), _full((1, D_MODEL)),
                  _mod_spec(0, TOK_TILE), _mod_spec(1, TOK_TILE),
                  _full((D_MODEL, C_IN_PAD)), _full((1, Q_LORA)), _full((1, KV_LORA)), _full((1, LANE)),
                  _full((Q_LORA, MLA_HEADS * MLA_PAD)), _full((KV_LORA, kv_n)),
                  _full((1, LANE)), _full((1, LANE)), tab, tab],
        out_specs=[pl.BlockSpec((TOK_TILE, MLA_HEADS * MLA_PAD), lambda i: (i, 0)),
                   pl.BlockSpec((None, TOK_TILE, MLA_HEADS * MLA_PAD), key_blk),
                   pl.BlockSpec((None, TOK_TILE, MLA_HEADS * MLA_V), key_blk)],
        out_shape=[jax.ShapeDtypeStruct((N_TOK, MLA_HEADS * MLA_PAD), BF16),
                   jax.ShapeDtypeStruct((BATCH, MLA_KEYS, MLA_HEADS * MLA_PAD), BF16),
                   jax.ShapeDtypeStruct((BATCH, MLA_KEYS, MLA_HEADS * MLA_V), BF16)],
        compiler_params=_params("parallel"),
        name="mla_proj",
    )(xt, g.reshape(1, D_MODEL), mods4, mods4, w_in, qa_g, kva_g, kr_g, w_q, w_kv, q_g, k_g, cos_t, sin_t)


MLA_GROUP = 4
MLA_Q_TILE = 256


def _mla_kernel(q_ref, k_ref, v_ref, *rest):
    o_ref = rest[-1]
    lane = lax.broadcasted_iota(jnp.int32, (q_ref.shape[0], 2 * MLA_V), 1)
    for jp in range(MLA_GROUP // 2):
        outs = []
        v2 = v_ref[:, jp * 2 * MLA_V:(jp + 1) * 2 * MLA_V]
        for j in range(2 * jp, 2 * jp + 2):
            qh = q_ref[:, j * MLA_PAD:(j + 1) * MLA_PAD]
            kh = k_ref[:, j * MLA_PAD:(j + 1) * MLA_PAD]
            s = _nt_dot(qh, kh)
            m = jnp.max(s, axis=-1, keepdims=True)
            p = jnp.exp2(s - m)
            l = jnp.sum(p, axis=-1, keepdims=True)
            outs.append(jnp.dot(p.astype(BF16), v2, preferred_element_type=F32) / l)
        o_ref[:, jp * 2 * MLA_V:(jp + 1) * 2 * MLA_V] = jnp.where(lane < MLA_V, outs[0], outs[1]).astype(o_ref.dtype)


def _mla_attn_latent(q, k, v, n_out_rows):
    nq = SEQ // MLA_Q_TILE
    qblk = lambda b, p, i: (b * nq + i, p)
    return pl.pallas_call(
        _mla_kernel,
        grid=(BATCH, MLA_HEADS // MLA_GROUP, nq),
        in_specs=[pl.BlockSpec((MLA_Q_TILE, MLA_GROUP * MLA_PAD), qblk),
                  pl.BlockSpec((None, MLA_KEYS, MLA_GROUP * MLA_PAD), lambda b, p, i: (b, 0, p)),
                  pl.BlockSpec((None, MLA_KEYS, MLA_GROUP * MLA_V), lambda b, p, i: (b, 0, p))],
        out_specs=pl.BlockSpec((MLA_Q_TILE, MLA_GROUP * MLA_V), qblk),
        out_shape=jax.ShapeDtypeStruct((n_out_rows, MLA_HEADS * MLA_V), BF16),
        compiler_params=_params("parallel", "parallel", "parallel"),
        name="mla_attn",
    )(q, k, v)


def _mla_attn_ctx(q, k, v, o):
    qblk = lambda b, p: (LAT_TILES + b, p)
    tail = lambda b, p: (b, TILES_PER_SEQ, p)
    return pl.pallas_call(
        _mla_kernel,
        grid=(BATCH, MLA_HEADS // MLA_GROUP),
        in_specs=[pl.BlockSpec((TOK_TILE, MLA_GROUP * MLA_PAD), qblk),
                  pl.BlockSpec((None, CTX_LEN, MLA_GROUP * MLA_PAD), tail),
                  pl.BlockSpec((None, CTX_LEN, MLA_GROUP * MLA_V), tail),
                  pl.BlockSpec(memory_space=pl.ANY)],
        out_specs=pl.BlockSpec((TOK_TILE, MLA_GROUP * MLA_V), qblk),
        out_shape=jax.ShapeDtypeStruct((N_TOK, MLA_HEADS * MLA_V), BF16),
        input_output_aliases={3: 0},
        compiler_params=_params("parallel", "parallel"),
        name="mla_attn_ctx",
    )(q, k, v, o)


def _outproj_kernel(n_in, *refs):
    a_refs = refs[:n_in]
    w_refs = refs[n_in:2 * n_in]
    x_ref, g1_ref, gn_ref, sh_ref, sc_ref, rw_ref, rb_ref, xo_ref, f_ref, lg_ref = refs[2 * n_in:]
    y = jnp.dot(a_refs[0][...], w_refs[0][...], preferred_element_type=F32)
    for a_ref, w_ref in zip(a_refs[1:], w_refs[1:]):
        y = y + jnp.dot(a_ref[...], w_ref[...], preferred_element_type=F32)
    xo_ref[...] = x_ref[...] + g1_ref[...] * y
    f = _modulated_norm(xo_ref, gn_ref, sh_ref, sc_ref)
    f_ref[...] = f
    lg_ref[...] = jnp.dot(f, rw_ref[...], precision=lax.Precision.HIGHEST,
                          preferred_element_type=F32) + rb_ref[...]


def _outproj(a_list, w_list, x, n_rows, mods4, gn, rw_pad, rb_pad):
    n_in = len(a_list)
    row = lambda n: pl.BlockSpec((OUT_TILE, n), lambda i: (i, 0))
    in_specs = [row(a.shape[1]) for a in a_list] + [_full(w.shape) for w in w_list]
    in_specs += [row(D_MODEL), _mod_spec(2, OUT_TILE), _full((1, D_MODEL)),
                 _mod_spec(3, OUT_TILE), _mod_spec(4, OUT_TILE), _full((D_MODEL, LANE)), _full((1, LANE))]
    return pl.pallas_call(
        functools.partial(_outproj_kernel, n_in),
        grid=(n_rows // OUT_TILE,),
        in_specs=in_specs,
        out_specs=[row(D_MODEL), row(D_MODEL), row(LANE)],
        out_shape=[jax.ShapeDtypeStruct((n_rows, D_MODEL), F32),
                   jax.ShapeDtypeStruct((n_rows, D_MODEL), F32),
                   jax.ShapeDtypeStruct((n_rows, LANE), F32)],
        compiler_params=_params("parallel"),
        name="outproj",
    )(*a_list, *w_list, x, mods4, gn.reshape(1, D_MODEL), mods4, mods4, rw_pad, rb_pad)


IDX_LANE = 0
GATE_LANE = TOP_K
POS_LANE = 2 * TOP_K
MASKED = -3.0e38


def _route_kernel(lg_ref, info_ref, cnt_ref, carry_ref):
    i = pl.program_id(0)

    @pl.when(i == 0)
    def _():
        carry_ref[...] = jnp.zeros_like(carry_ref)

    lane = lax.broadcasted_iota(jnp.int32, (TOK_TILE, LANE), 1).astype(F32)
    x = jnp.where(lane < N_EXPERTS, lg_ref[...], MASKED)
    vals, idxs = [], []
    for _ in range(TOP_K):
        m = jnp.max(x, axis=-1, keepdims=True)
        idx = jnp.min(jnp.where(x == m, lane, float(LANE)), axis=-1, keepdims=True)
        vals.append(m)
        idxs.append(idx)
        x = jnp.where(lane == idx, MASKED, x)
    es = [jnp.exp(v - vals[0]) for v in vals]
    denom = es[0] + es[1] + es[2] + es[3]
    onehot = jnp.zeros((TOK_TILE, LANE), F32)
    for idx in idxs:
        onehot = onehot + jnp.where(lane == idx, 1.0, 0.0)
    r = lax.broadcasted_iota(jnp.int32, (TOK_TILE, TOK_TILE), 0)
    c = lax.broadcasted_iota(jnp.int32, (TOK_TILE, TOK_TILE), 1)
    tri = jnp.where(r > c, 1.0, 0.0).astype(BF16)
    before = jnp.dot(tri, onehot.astype(BF16), preferred_element_type=F32) + carry_ref[...]
    info = jnp.zeros((TOK_TILE, LANE), F32)
    for k in range(TOP_K):
        pos = jnp.sum(jnp.where(lane == idxs[k], before, 0.0), axis=-1, keepdims=True)
        info = jnp.where(lane == IDX_LANE + k, idxs[k], info)
        info = jnp.where(lane == GATE_LANE + k, es[k] / denom, info)
        info = jnp.where(lane == POS_LANE + k, pos, info)
    info_ref[...] = info
    carry_ref[...] = carry_ref[...] + jnp.sum(onehot, axis=0, keepdims=True)
    cnt_ref[...] = carry_ref[...]


def _route(logits, n_rows):
    return pl.pallas_call(
        _route_kernel,
        grid=(n_rows // TOK_TILE,),
        in_specs=[pl.BlockSpec((TOK_TILE, LANE), lambda i: (i, 0))],
        out_specs=[pl.BlockSpec((TOK_TILE, LANE), lambda i: (i, 0)), _full((1, LANE))],
        out_shape=[jax.ShapeDtypeStruct((n_rows, LANE), F32), jax.ShapeDtypeStruct((1, LANE), F32)],
        scratch_shapes=[pltpu.VMEM((1, LANE), F32)],
        compiler_params=_params("arbitrary"),
        name="route",
    )(logits)


def _row_copy(src_ref, src_row, dst_ref, dst_row, sem):
    return pltpu.make_async_copy(src_ref.at[pl.ds(src_row, 1)], dst_ref.at[pl.ds(dst_row, 1)], sem)


def _dispatch_kernel(dest_ref, f_ref, xb_ref, sem):
    def issue(r, carry):
        for k in range(TOP_K):
            _row_copy(f_ref, r, xb_ref, dest_ref[0, r * TOP_K + k], sem).start()
        return carry

    lax.fori_loop(0, TOK_TILE, issue, 0, unroll=8)
    for _ in range(TOP_K):
        pltpu.make_async_copy(f_ref, xb_ref.at[pl.ds(0, TOK_TILE)], sem).wait()


def _dispatch(dest3, f, n_rows_out):
    n_rows = dest3.shape[0] * TOK_TILE
    return pl.pallas_call(
        _dispatch_kernel,
        grid=(n_rows // TOK_TILE,),
        in_specs=[pl.BlockSpec((None, 1, TOK_TILE * TOP_K), lambda i: (i, 0, 0), memory_space=pltpu.SMEM),
                  pl.BlockSpec((TOK_TILE, D_MODEL), lambda i: (i, 0))],
        out_specs=pl.BlockSpec(memory_space=pl.ANY),
        out_shape=jax.ShapeDtypeStruct((n_rows_out, D_MODEL), F32),
        scratch_shapes=[pltpu.SemaphoreType.DMA(())],
        compiler_params=_params("arbitrary"),
        name="moe_dispatch",
    )(dest3, f)


def _moe_kernel(layer, be_ref, nv_ref, first_ref, nxt_ref, x_ref, wgu_hbm, bgu_ref, wd_hbm, bd_ref, o_ref,
                wgu_f32, wd_f32, wgu_s, wd_s, sems):
    i = pl.program_id(0)

    def fetch(e):
        return (pltpu.make_async_copy(wgu_hbm.at[layer, e], wgu_f32, sems.at[0]),
                pltpu.make_async_copy(wd_hbm.at[layer, e], wd_f32, sems.at[1]))

    @pl.when(i == 0)
    def _():
        for cp in fetch(be_ref[0]):
            cp.start()

    @pl.when(first_ref[i] == 1)
    def _():
        for cp in fetch(be_ref[i]):
            cp.wait()
        wgu_s[...] = wgu_f32[...].astype(BF16)
        wd_s[...] = wd_f32[...].astype(BF16)

        @pl.when(nxt_ref[i] >= 0)
        def _():
            for cp in fetch(nxt_ref[i]):
                cp.start()

    @pl.when(nv_ref[i] == 0)
    def _():
        o_ref[...] = jnp.zeros_like(o_ref)

    @pl.when(nv_ref[i] > 0)
    def _():
        row = lax.broadcasted_iota(jnp.int32, (MOE_ROWS, D_MODEL), 0)
        x = jnp.where(row < nv_ref[i], x_ref[...], 0.0).astype(BF16)
        gu = jnp.dot(x, wgu_s[...], preferred_element_type=F32) + bgu_ref[...]
        gate = jnp.minimum(gu[:, :D_EXPERT], SWIGLU_LIMIT)
        up = jnp.clip(gu[:, D_EXPERT:], -SWIGLU_LIMIT, SWIGLU_LIMIT)
        act = (up + 1.0) * (gate * jax.nn.sigmoid(SWIGLU_ALPHA * gate))
        o_ref[...] = jnp.dot(act.astype(BF16), wd_s[...], preferred_element_type=F32) + bd_ref[...]


def _moe_experts(layer, block_expert, rows_valid, first, nxt, xb, w_gu, b_gu, w_down, b_down):
    n_rows = xb.shape[0]
    bsel = lambda i, be, nv, fi, nx: (layer, be[i], 0, 0)
    rows = lambda i, be, nv, fi, nx: (i, 0)
    grid_spec = pltpu.PrefetchScalarGridSpec(
        num_scalar_prefetch=4,
        grid=(n_rows // MOE_ROWS,),
        in_specs=[
            pl.BlockSpec((MOE_ROWS, D_MODEL), rows),
            pl.BlockSpec(memory_space=pl.ANY),
            pl.BlockSpec((None, None, 1, 2 * D_EXPERT), bsel),
            pl.BlockSpec(memory_space=pl.ANY),
            pl.BlockSpec((None, None, 1, D_MODEL), bsel),
        ],
        out_specs=pl.BlockSpec((MOE_ROWS, D_MODEL), rows),
        scratch_shapes=[pltpu.VMEM((D_MODEL, 2 * D_EXPERT), F32),
                        pltpu.VMEM((D_EXPERT, D_MODEL), F32),
                        pltpu.VMEM((D_MODEL, 2 * D_EXPERT), BF16),
                        pltpu.VMEM((D_EXPERT, D_MODEL), BF16),
                        pltpu.SemaphoreType.DMA((2,))],
    )
    return pl.pallas_call(
        functools.partial(_moe_kernel, layer),
        grid_spec=grid_spec,
        out_shape=jax.ShapeDtypeStruct((n_rows, D_MODEL), F32),
        compiler_params=_params("arbitrary"),
        name="moe_experts",
    )(block_expert, rows_valid, first, nxt, xb, w_gu, b_gu.reshape(DEPTH, N_EXPERTS, 1, -1), w_down,
      b_down.reshape(DEPTH, N_EXPERTS, 1, -1))


def _combine_kernel(dest_ref, x_ref, g2_ref, info_ref, yb_ref, o_ref, buf, sem):
    def issue(r, carry):
        for k in range(TOP_K):
            _row_copy(yb_ref, dest_ref[0, r * TOP_K + k], buf.at[k], r, sem).start()
        return carry

    lax.fori_loop(0, TOK_TILE, issue, 0, unroll=8)
    for k in range(TOP_K):
        pltpu.make_async_copy(yb_ref.at[pl.ds(0, TOK_TILE)], buf.at[k], sem).wait()
    info = info_ref[...]
    y = info[:, GATE_LANE:GATE_LANE + 1] * buf[0]
    for k in range(1, TOP_K):
        y = y + info[:, GATE_LANE + k:GATE_LANE + k + 1] * buf[k]
    o_ref[...] = x_ref[...] + g2_ref[...] * y


def _combine(dest3, x, mods4, info, yb):
    n_rows = dest3.shape[0] * TOK_TILE
    row = lambda n: pl.BlockSpec((TOK_TILE, n), lambda i: (i, 0))
    return pl.pallas_call(
        _combine_kernel,
        grid=(n_rows // TOK_TILE,),
        in_specs=[pl.BlockSpec((None, 1, TOK_TILE * TOP_K), lambda i: (i, 0, 0), memory_space=pltpu.SMEM),
                  row(D_MODEL), _mod_spec(5, TOK_TILE), row(LANE), pl.BlockSpec(memory_space=pl.ANY)],
        out_specs=row(D_MODEL),
        out_shape=jax.ShapeDtypeStruct((n_rows, D_MODEL), F32),
        scratch_shapes=[pltpu.VMEM((TOP_K, TOK_TILE, D_MODEL), F32), pltpu.SemaphoreType.DMA(())],
        compiler_params=_params("arbitrary"),
        name="moe_combine",
    )(dest3, x, mods4, info, yb)


def _moe_layer(layer, x_new, f, logits, n_rows, mods4, w_gu, b_gu, w_down, b_down):
    info, cnt = _route(logits, n_rows)
    idx = info[:, IDX_LANE:IDX_LANE + TOP_K].astype(jnp.int32)
    pos = info[:, POS_LANE:POS_LANE + TOP_K].astype(jnp.int32)
    counts = cnt[0, :N_EXPERTS].astype(jnp.int32)
    padded = (counts + MOE_ROWS - 1) // MOE_ROWS * MOE_ROWS
    pad_ends = jnp.cumsum(padded)
    pad_starts = pad_ends - padded
    experts = jnp.arange(N_EXPERTS, dtype=jnp.int32)
    dest = pos + jnp.sum(jnp.where(idx[:, :, None] == experts, pad_starts, 0), axis=-1)
    dest3 = dest.reshape(n_rows // TOK_TILE, 1, TOK_TILE * TOP_K)
    n_blocks = n_rows * TOP_K // MOE_ROWS + N_EXPERTS
    block_start = jnp.arange(n_blocks, dtype=jnp.int32) * MOE_ROWS
    used = counts > 0
    last_used = jnp.max(jnp.where(used, experts, 0))
    block_expert = jnp.minimum(jnp.sum((block_start[:, None] >= pad_ends[None, :]).astype(jnp.int32), axis=1),
                               last_used)
    group_end = (pad_starts + counts)[block_expert]
    rows_valid = jnp.clip(group_end - block_start, 0, MOE_ROWS).astype(jnp.int32)
    first = jnp.concatenate([jnp.ones((1,), jnp.int32),
                             (block_expert[1:] != block_expert[:-1]).astype(jnp.int32)])
    later = jnp.where(used[None, :] & (experts[None, :] > experts[:, None]), experts[None, :], N_EXPERTS)
    next_used = jnp.min(later, axis=1)
    nxt = jnp.where(next_used < N_EXPERTS, next_used, -1)[block_expert].astype(jnp.int32)
    xb = _dispatch(dest3, f, n_blocks * MOE_ROWS)
    yb = _moe_experts(layer, block_expert, rows_valid, first, nxt, xb, w_gu, b_gu, w_down, b_down)
    return _combine(dest3, x_new, mods4, info, yb)


def _pad_lanes(v, start=0):
    return jnp.zeros((1, LANE), F32).at[0, start:start + v.shape[0]].set(v)


def kernel(x, c, ctx, c_ctx, mod_w, mod_b, norm_mix_g, norm_ffn_g, ab_w_in, ab_w_out, a_q_norm, a_k_norm, a_sink, b_dw_w, b_dw_b, b_ln_g, b_ln_b, c_w_in, c_q_a_norm, c_kv_a_norm, c_w_q_b, c_w_kv_b, c_q_norm, c_k_norm, c_kr_norm, c_w_out, router_w, router_b, exp_w_gu, exp_b_gu, exp_w_down, exp_b_down):
    cos_a, sin_a = _rope_lane_tables(HEAD_DIM, 0, LANE // HEAD_DIM)
    cos_c, sin_c = _rope_lane_tables(MLA_ROPE, MLA_NOPE, 1)
    cvec = jnp.concatenate([c, c_ctx[None, :], jnp.zeros((SEG_PAD - N_SEG, D_MODEL), F32)], axis=0)
    mods = _modulation(cvec, mod_w, mod_b)
    xt = jnp.concatenate([x.reshape(N_LAT, D_MODEL), ctx.reshape(N_CTX, D_MODEL)], axis=0)
    for layer in range(DEPTH):
        keep_ctx = layer < DEPTH - 1
        n_rows = N_TOK if keep_ctx else N_LAT
        mods4 = mods[layer].reshape(SEG_PAD, 6, 1, D_MODEL)
        i = layer // 2
        if layer % 2 == 0:
            qg2 = jnp.tile(a_q_norm[i], LANE // HEAD_DIM).reshape(1, LANE)
            kg2 = jnp.tile(a_k_norm[i], LANE // HEAD_DIM).reshape(1, LANE)
            q, k, v, z = _even_inproj(xt, norm_mix_g[layer], mods4, ab_w_in[i].astype(BF16), qg2, kg2, cos_a, sin_a)
            sink_b = jnp.broadcast_to(a_sink[i].astype(F32)[:, None], (A_Q_HEADS, LANE))
            a = _win_attn(q, k, v, sink_b)
            b = _conv_module(z, b_dw_w[i], b_dw_b[i], b_ln_g[i], b_ln_b[i])
            w_out = ab_w_out[i].astype(BF16)
            a_list, w_list = [a, b], [w_out[:DQ], w_out[DQ:]]
        else:
            w_in = c_w_in[i]
            zeros = lambda n: jnp.zeros((D_MODEL, n), F32)
            w_in_p = jnp.concatenate([w_in[:, :Q_LORA + KV_LORA], zeros(MLA_NOPE), w_in[:, Q_LORA + KV_LORA:],
                                      zeros(LANE - MLA_QK)], axis=1).astype(BF16)
            w_q = jnp.pad(c_w_q_b[i].reshape(Q_LORA, MLA_HEADS, MLA_QK),
                          ((0, 0), (0, 0), (0, MLA_PAD - MLA_QK))).reshape(Q_LORA, MLA_HEADS * MLA_PAD)
            w_kv = c_w_kv_b[i].reshape(KV_LORA, MLA_HEADS, MLA_NOPE + MLA_V)
            w_k = jnp.pad(w_kv[:, :, :MLA_NOPE], ((0, 0), (0, 0), (0, MLA_PAD - MLA_NOPE)))
            w_kv_p = jnp.concatenate([w_k.reshape(KV_LORA, -1), w_kv[:, :, MLA_NOPE:].reshape(KV_LORA, -1)], axis=1)
            q, k, v = _mla_proj(xt, norm_mix_g[layer], mods4, w_in_p, c_q_a_norm[i].reshape(1, -1),
                                c_kv_a_norm[i].reshape(1, -1), _pad_lanes(c_kr_norm[i], MLA_NOPE),
                                w_q.astype(BF16), w_kv_p.astype(BF16), _pad_lanes(c_q_norm[i]),
                                _pad_lanes(c_k_norm[i]), cos_c, sin_c)
            o = _mla_attn_latent(q, k, v, n_rows)
            if keep_ctx:
                o = _mla_attn_ctx(q, k, v, o)
            a_list, w_list = [o], [c_w_out[i].astype(BF16)]
        rw_pad = jnp.pad(router_w[layer], ((0, 0), (0, LANE - N_EXPERTS)))
        rb_pad = jnp.pad(router_b[layer], (0, LANE - N_EXPERTS)).reshape(1, LANE)
        x_new, f, logits = _outproj(a_list, w_list, xt, n_rows, mods4, norm_ffn_g[layer], rw_pad, rb_pad)
        xt = _moe_layer(layer, x_new, f, logits, n_rows, mods4, exp_w_gu, exp_b_gu, exp_w_down, exp_b_down)
    return xt.reshape(BATCH, SEQ, D_MODEL)
```

```python
import functools

import jax
import jax.numpy as jnp
import numpy as np
from jax import lax
from jax.experimental import pallas as pl
from jax.experimental.pallas import tpu as pltpu

D_MODEL = 1024
BATCH = 4
SEQ = 4096
DEPTH = 4
GRID_W = 64
CTX_LEN = 256
A_Q_HEADS = 8
A_KV_HEADS = 2
HEAD_DIM = 64
WINDOW = 128
CONV_CH = 512
CONV_WIDTH = 31
MLA_HEADS = 16
MLA_NOPE = 64
MLA_ROPE = 32
MLA_V = 64
MLA_QK = MLA_NOPE + MLA_ROPE
Q_LORA = 768
KV_LORA = 256
N_EXPERTS = 32
TOP_K = 4
D_EXPERT = 1024
SWIGLU_LIMIT = 7.0
SWIGLU_ALPHA = 1.702
ROPE_THETA = 10000.0
EPS = 1e-6
NEG_INF = -1e30

LANE = 128
N_LAT = BATCH * SEQ
N_CTX = BATCH * CTX_LEN
N_TOK = N_LAT + N_CTX
N_SEG = BATCH + 1
SEG_PAD = 8
DQ = A_Q_HEADS * HEAD_DIM
DKV = A_KV_HEADS * HEAD_DIM
DKV_DUP = A_KV_HEADS * LANE
AB_IN = DQ + 2 * DKV + 2 * CONV_CH
MLA_PAD = LANE
C_IN_PAD = Q_LORA + KV_LORA + LANE
MLA_KEYS = SEQ + CTX_LEN

TOK_TILE = 256
LAT_TILES = N_LAT // TOK_TILE
TILES_PER_SEQ = SEQ // TOK_TILE
OUT_TILE = 512
MOE_ROWS = 256
CONV_HALO = 16
VMEM_LIMIT = 56 * 1024 * 1024

F32 = jnp.float32
BF16 = jnp.bfloat16


def _params(*sem):
    return pltpu.CompilerParams(dimension_semantics=sem, vmem_limit_bytes=VMEM_LIMIT)


def _seg_of_tile(i, tile):
    return jnp.minimum(i * tile // SEQ, BATCH)


def _rope_block_of_tile(i):
    return jnp.where(i < LAT_TILES, i % TILES_PER_SEQ, TILES_PER_SEQ)


def _mod_spec(chunk, tile):
    return pl.BlockSpec((None, None, 1, D_MODEL), lambda i: (_seg_of_tile(i, tile), chunk, 0, 0))


def _full(shape):
    return pl.BlockSpec(shape, lambda *_: (0,) * len(shape))


def _rms(x, n=None):
    n = x.shape[-1] if n is None else n
    return x * lax.rsqrt(jnp.sum(x * x, axis=-1, keepdims=True) * (1.0 / n) + EPS)


def _modulated_norm(x_ref, g_ref, sh_ref, sc_ref):
    h = _rms(x_ref[...]) * g_ref[...]
    return h * (1.0 + sc_ref[...]) + sh_ref[...]


def _mod_kernel(c_ref, w_ref, b_ref, o_ref):
    c = c_ref[...]
    s = c * jax.nn.sigmoid(c)
    o_ref[...] = jnp.dot(s, w_ref[...], precision=lax.Precision.HIGHEST,
                         preferred_element_type=F32) + b_ref[...]


def _modulation(cvec, mod_w, mod_b):
    tn = 1536
    return pl.pallas_call(
        _mod_kernel,
        grid=(DEPTH, 6 * D_MODEL // tn),
        in_specs=[
            pl.BlockSpec((SEG_PAD, D_MODEL), lambda l, j: (0, 0)),
            pl.BlockSpec((None, D_MODEL, tn), lambda l, j: (l, 0, j)),
            pl.BlockSpec((None, 1, tn), lambda l, j: (l, 0, j)),
        ],
        out_specs=pl.BlockSpec((None, SEG_PAD, tn), lambda l, j: (l, 0, j)),
        out_shape=jax.ShapeDtypeStruct((DEPTH, SEG_PAD, 6 * D_MODEL), F32),
        compiler_params=_params("arbitrary", "arbitrary"),
        name="modulation",
    )(cvec, mod_w, mod_b.reshape(DEPTH, 1, 6 * D_MODEL))


def _rope_lanes(t, cos, sin, half0, shift):
    partner = jnp.where(half0, pltpu.roll(t, LANE - shift, 1), pltpu.roll(t, shift, 1))
    return t * cos + partner * sin


def _rope_lane_tables(rot_dims, lane_start, repeat):
    f32 = np.float32
    rows = SEQ // GRID_W
    row = np.repeat(np.arange(rows), GRID_W).astype(f32)
    col = np.tile(np.arange(GRID_W), rows).astype(f32)
    axis_dims = rot_dims // 2
    n = axis_dims // 2
    inv = (f32(ROPE_THETA) ** (-(np.arange(n, dtype=f32) * f32(2.0)) / f32(axis_dims))).astype(f32)
    ang = np.concatenate([row[:, None] * inv, col[:, None] * inv], axis=-1)
    d = np.arange(rot_dims)
    src = (d // (2 * n)) * n + d % n
    sign = np.where((d % (2 * n)) // n == 0, -1.0, 1.0).astype(f32)
    cos_t = np.ones((SEQ + TOK_TILE, LANE), f32)
    sin_t = np.zeros((SEQ + TOK_TILE, LANE), f32)
    for r in range(repeat):
        lo = lane_start + r * rot_dims
        cos_t[:SEQ, lo:lo + rot_dims] = np.cos(ang)[:, src]
        sin_t[:SEQ, lo:lo + rot_dims] = np.sin(ang)[:, src] * sign
    return jnp.asarray(cos_t), jnp.asarray(sin_t)


def _even_inproj_kernel(x_ref, g_ref, sh_ref, sc_ref, w_ref, qg_ref, kg_ref, cos_ref, sin_ref,
                        q_ref, k_ref, v_ref, z_ref):
    h = _modulated_norm(x_ref, g_ref, sh_ref, sc_ref)
    p = jnp.dot(h.astype(BF16), w_ref[...], preferred_element_type=F32)
    lane = lax.broadcasted_iota(jnp.int32, (TOK_TILE, LANE), 1)
    lo = lane < HEAD_DIM
    half0 = (lane & (HEAD_DIM // 2 - 1)) < HEAD_DIM // 4
    cos = cos_ref[...]
    sin = sin_ref[...]

    def norm_rope(t, gvec):
        ss = t * t
        s_lo = jnp.sum(jnp.where(lo, ss, 0.0), axis=-1, keepdims=True)
        s_hi = jnp.sum(jnp.where(lo, 0.0, ss), axis=-1, keepdims=True)
        ms = jnp.where(lo, s_lo, s_hi) * (1.0 / HEAD_DIM)
        tn = t * lax.rsqrt(ms + EPS) * gvec
        return _rope_lanes(tn, cos, sin, half0, HEAD_DIM // 4)

    scale = HEAD_DIM ** -0.5
    for j in range(DQ // LANE):
        q_ref[:, j * LANE:(j + 1) * LANE] = (norm_rope(p[:, j * LANE:(j + 1) * LANE], qg_ref[...]) * scale).astype(BF16)
    def twice(t, ref):
        sw = pltpu.roll(t, HEAD_DIM, 1)
        ref[:, :LANE] = jnp.where(lo, t, sw).astype(BF16)
        ref[:, LANE:] = jnp.where(lo, sw, t).astype(BF16)

    twice(norm_rope(p[:, DQ:DQ + DKV], kg_ref[...]), k_ref)
    twice(p[:, DQ + DKV:DQ + 2 * DKV], v_ref)
    u0 = DQ + 2 * DKV
    z_ref[...] = (p[:, u0:u0 + CONV_CH] * jax.nn.sigmoid(p[:, u0 + CONV_CH:])).astype(BF16)


def _even_inproj(xt, g, mods4, w_bf, qg2, kg2, cos_t, sin_t):
    row = lambda n: pl.BlockSpec((TOK_TILE, n), lambda i: (i, 0))
    tab = pl.BlockSpec((TOK_TILE, LANE), lambda i: (_rope_block_of_tile(i), 0))
    return pl.pallas_call(
        _even_inproj_kernel,
        grid=(N_TOK // TOK_TILE,),
        in_specs=[row(D_MODEL), _full((1, D_MODEL)), _mod_spec(0, TOK_TILE), _mod_spec(1, TOK_TILE),
                  _full((D_MODEL, AB_IN)), _full((1, LANE)), _full((1, LANE)), tab, tab],
        out_specs=[row(DQ), row(DKV_DUP), row(DKV_DUP), row(CONV_CH)],
        out_shape=[jax.ShapeDtypeStruct((N_TOK, DQ), BF16),
                   jax.ShapeDtypeStruct((N_TOK, DKV_DUP), BF16),
                   jax.ShapeDtypeStruct((N_TOK, DKV_DUP), BF16),
                   jax.ShapeDtypeStruct((N_TOK, CONV_CH), BF16)],
        compiler_params=_params("parallel"),
        name="even_inproj",
    )(xt, g.reshape(1, D_MODEL), mods4, mods4, w_bf, qg2, kg2, cos_t, sin_t)


def _softmax_pv(s, sk, v):
    m = jnp.maximum(jnp.max(s, axis=-1, keepdims=True), sk)
    p = jnp.exp(s - m)
    l = jnp.sum(p, axis=-1, keepdims=True) + jnp.exp(sk - m)
    o = jnp.dot(p.astype(BF16), v, preferred_element_type=F32)
    return o / l


def _nt_dot(a, b):
    return lax.dot_general(a, b, (((1,), (1,)), ((), ())), preferred_element_type=F32)


N_WIN = SEQ // WINDOW
CTX_QBLOCKS = CTX_LEN // WINDOW


def _win_attn_kernel(q_ref, kp_ref, kc_ref, kn_ref, vp_ref, vc_ref, vn_ref, kx_ref, vx_ref,
                     sink_ref, o_ref):
    n = pl.program_id(1)
    w = WINDOW
    k_all = jnp.concatenate([kp_ref[...], kc_ref[...], kn_ref[...], kx_ref[...]], axis=0)
    v_all = jnp.concatenate([vp_ref[...], vc_ref[...], vn_ref[...], vx_ref[...]], axis=0)
    nk = 3 * w + CTX_LEN
    group = A_Q_HEADS // A_KV_HEADS
    row = lax.broadcasted_iota(jnp.int32, (group * w, nk), 0)
    qi = row & (w - 1)
    ki = lax.broadcasted_iota(jnp.int32, (group * w, nk), 1)
    rel = ki - w - qi
    kpos = (n - 1) * w + ki
    kpos_end = jnp.where(n < N_WIN, SEQ, -1)
    valid = (ki >= 3 * w) | ((jnp.abs(rel) <= w) & (kpos >= 0) & (kpos < kpos_end))
    lo = lax.broadcasted_iota(jnp.int32, (w, LANE), 1) < HEAD_DIM
    head_of_row = lax.broadcasted_iota(jnp.int32, (group * w, 1), 0) // w
    tiles_per_kv = group * HEAD_DIM // LANE
    for g in range(A_KV_HEADS):
        kk = k_all[:, g * LANE:(g + 1) * LANE]
        vv = v_all[:, g * LANE:(g + 1) * LANE]
        parts = []
        for j in range(g * tiles_per_kv, (g + 1) * tiles_per_kv):
            qt = q_ref[:, j * LANE:(j + 1) * LANE]
            zero = jnp.zeros_like(qt)
            parts += [jnp.where(lo, qt, zero), jnp.where(lo, zero, qt)]
        qs = jnp.concatenate(parts, axis=0)
        sk = sink_ref[g * group:g * group + 1, 0:1]
        for hh in range(1, group):
            sk = jnp.where(head_of_row == hh, sink_ref[g * group + hh:g * group + hh + 1, 0:1], sk)
        s = jnp.where(valid, _nt_dot(qs, kk), NEG_INF)
        o = _softmax_pv(s, sk, vv)
        for t in range(tiles_per_kv):
            j = g * tiles_per_kv + t
            o_ref[:, j * LANE:(j + 1) * LANE] = jnp.where(
                lo, o[2 * t * w:(2 * t + 1) * w], o[(2 * t + 1) * w:(2 * t + 2) * w]).astype(o_ref.dtype)


def _win_attn(q, k, v, sink_b):
    def qblk(b, n):
        return (jnp.where(n < N_WIN, b * N_WIN + n, BATCH * N_WIN + b * CTX_QBLOCKS + (n - N_WIN)), 0)

    def local(off):
        return lambda b, n: (b * N_WIN + jnp.clip(n + off, 0, N_WIN - 1), 0)

    kv_spec = lambda f: pl.BlockSpec((WINDOW, DKV_DUP), f)
    ctx_spec = pl.BlockSpec((CTX_LEN, DKV_DUP), lambda b, n: (LAT_TILES + b, 0))
    return pl.pallas_call(
        _win_attn_kernel,
        grid=(BATCH, N_WIN + CTX_QBLOCKS),
        in_specs=[pl.BlockSpec((WINDOW, DQ), qblk),
                  kv_spec(local(-1)), kv_spec(local(0)), kv_spec(local(1)),
                  kv_spec(local(-1)), kv_spec(local(0)), kv_spec(local(1)),
                  ctx_spec, ctx_spec, _full((A_Q_HEADS, LANE))],
        out_specs=pl.BlockSpec((WINDOW, DQ), qblk),
        out_shape=jax.ShapeDtypeStruct((N_TOK, DQ), BF16),
        compiler_params=_params("parallel", "parallel"),
        name="win_attn",
    )(q, k, k, k, v, v, v, k, v, sink_b)


F32_SUBLANES = 8


def _conv_kernel(zp_ref, zc_ref, zn_ref, w_ref, b_ref, g_ref, bb_ref, o_ref, pad_ref, shift_ref):
    i = pl.program_id(0)
    tl = TOK_TILE
    h = CONV_HALO
    pos = i % TILES_PER_SEQ
    first = (i >= LAT_TILES) | (pos == 0)
    last = (i >= LAT_TILES) | (pos == TILES_PER_SEQ - 1)
    pad_ref[0:h, :] = jnp.where(first, 0.0, zp_ref[...].astype(F32))
    pad_ref[h:h + tl, :] = zc_ref[...].astype(F32)
    pad_ref[h + tl:2 * h + tl, :] = jnp.where(last, 0.0, zn_ref[...].astype(F32))
    span = tl + 2 * h - F32_SUBLANES
    for r in range(1, F32_SUBLANES):
        shift_ref[r - 1] = pad_ref[r:r + span, :]

    def tap(j):
        start = h - CONV_WIDTH // 2 + j
        r = start % F32_SUBLANES
        src = pad_ref if r == 0 else shift_ref.at[r - 1]
        return src[start - r:start - r + tl, :] * w_ref[j:j + 1, :]

    acc = tap(0)
    for j in range(1, CONV_WIDTH):
        acc = acc + tap(j)
    z = acc + b_ref[...]
    mu = jnp.mean(z, axis=-1, keepdims=True)
    zc = z - mu
    y = zc * lax.rsqrt(jnp.mean(zc * zc, axis=-1, keepdims=True) + EPS)
    y = y * g_ref[...] + bb_ref[...]
    o_ref[...] = (y * jax.nn.sigmoid(y)).astype(o_ref.dtype)


def _conv_module(z, dw_w, dw_b, ln_g, ln_b):
    r = TOK_TILE // CONV_HALO
    n_halo = N_TOK // CONV_HALO
    halo = lambda f: pl.BlockSpec((CONV_HALO, CONV_CH), f)
    vec = _full((1, CONV_CH))
    return pl.pallas_call(
        _conv_kernel,
        grid=(N_TOK // TOK_TILE,),
        in_specs=[halo(lambda i: (jnp.maximum(i * r - 1, 0), 0)),
                  pl.BlockSpec((TOK_TILE, CONV_CH), lambda i: (i, 0)),
                  halo(lambda i: (jnp.minimum((i + 1) * r, n_halo - 1), 0)),
                  _full((CONV_WIDTH, CONV_CH)), vec, vec, vec],
        out_specs=pl.BlockSpec((TOK_TILE, CONV_CH), lambda i: (i, 0)),
        out_shape=jax.ShapeDtypeStruct((N_TOK, CONV_CH), BF16),
        scratch_shapes=[pltpu.VMEM((TOK_TILE + 2 * CONV_HALO, CONV_CH), F32),
                        pltpu.VMEM((F32_SUBLANES - 1, TOK_TILE + 2 * CONV_HALO - F32_SUBLANES, CONV_CH), F32)],
        compiler_params=_params("parallel"),
        name="conv_module",
    )(z, z, z, dw_w, dw_b.reshape(1, -1), ln_g.reshape(1, -1), ln_b.reshape(1, -1))


def _mla_proj_kernel(x_ref, g_ref, sh_ref, sc_ref, win_ref, qag_ref, kvag_ref, krg_ref, wq_ref, wkv_ref,
                     qg_ref, kg_ref, cos_ref, sin_ref, q_ref, k_ref, v_ref):
    h = _modulated_norm(x_ref, g_ref, sh_ref, sc_ref)
    p = jnp.dot(h.astype(BF16), win_ref[...], preferred_element_type=F32)
    lane = lax.broadcasted_iota(jnp.int32, (TOK_TILE, LANE), 1)
    half0 = (lane & (MLA_ROPE // 2 - 1)) < MLA_ROPE // 4
    cos = cos_ref[...]
    sin = sin_ref[...]
    rope = lambda t: _rope_lanes(t, cos, sin, half0, MLA_ROPE // 4)

    cq = _rms(p[:, :Q_LORA]) * qag_ref[...]
    ckv = _rms(p[:, Q_LORA:Q_LORA + KV_LORA]) * kvag_ref[...]
    kr = rope(_rms(p[:, Q_LORA + KV_LORA:], MLA_ROPE) * krg_ref[...])

    q = jnp.dot(cq.astype(BF16), wq_ref[...], preferred_element_type=F32)
    scale = float(MLA_QK ** -0.5 * np.log2(np.e))
    for hd in range(MLA_HEADS):
        qh = _rms(q[:, hd * MLA_PAD:(hd + 1) * MLA_PAD], MLA_QK) * qg_ref[...]
        q_ref[:, hd * MLA_PAD:(hd + 1) * MLA_PAD] = (rope(qh) * scale).astype(BF16)

    kv = jnp.dot(ckv.astype(BF16), wkv_ref[...], preferred_element_type=F32)
    for hd in range(MLA_HEADS):
        kh = _rms(kv[:, hd * MLA_PAD:(hd + 1) * MLA_PAD], MLA_NOPE) * kg_ref[...]
        k_ref[:, hd * MLA_PAD:(hd + 1) * MLA_PAD] = (kh + kr).astype(BF16)
    v_ref[...] = kv[:, MLA_HEADS * MLA_PAD:].astype(BF16)


def _mla_proj(xt, g, mods4, w_in, qa_g, kva_g, kr_g, w_q, w_kv, q_g, k_g, cos_t, sin_t):
    tab = pl.BlockSpec((TOK_TILE, LANE), lambda i: (_rope_block_of_tile(i), 0))

    def key_blk(i):
        return (jnp.where(i < LAT_TILES, i // TILES_PER_SEQ, i - LAT_TILES), _rope_block_of_tile(i), 0)

    kv_n = MLA_HEADS * (MLA_PAD + MLA_V)
    return pl.pallas_call(
        _mla_proj_kernel,
        grid=(N_TOK // TOK_TILE,),
        in_specs=[pl.BlockSpec((TOK_TILE, D_MODEL), lambda i: (i, 0)), _full((1, D_MODEL)),
                  _mod_spec(0, TOK_TILE), _mod_spec(1, TOK_TILE),
                  _full((D_MODEL, C_IN_PAD)), _full((1, Q_LORA)), _full((1, KV_LORA)), _full((1, LANE)),
                  _full((Q_LORA, MLA_HEADS * MLA_PAD)), _full((KV_LORA, kv_n)),
                  _full((1, LANE)), _full((1, LANE)), tab, tab],
        out_specs=[pl.BlockSpec((TOK_TILE, MLA_HEADS * MLA_PAD), lambda i: (i, 0)),
                   pl.BlockSpec((None, TOK_TILE, MLA_HEADS * MLA_PAD), key_blk),
                   pl.BlockSpec((None, TOK_TILE, MLA_HEADS * MLA_V), key_blk)],
        out_shape=[jax.ShapeDtypeStruct((N_TOK, MLA_HEADS * MLA_PAD), BF16),
                   jax.ShapeDtypeStruct((BATCH, MLA_KEYS, MLA_HEADS * MLA_PAD), BF16),
                   jax.ShapeDtypeStruct((BATCH, MLA_KEYS, MLA_HEADS * MLA_V), BF16)],
        compiler_params=_params("parallel"),
        name="mla_proj",
    )(xt, g.reshape(1, D_MODEL), mods4, mods4, w_in, qa_g, kva_g, kr_g, w_q, w_kv, q_g, k_g, cos_t, sin_t)


MLA_GROUP = 8
MLA_Q_TILE = 256


def _mla_kernel(q_ref, k_ref, v_ref, *rest):
    o_ref = rest[-1]
    lane = lax.broadcasted_iota(jnp.int32, (q_ref.shape[0], 2 * MLA_V), 1)
    for jp in range(MLA_GROUP // 2):
        outs = []
        v2 = v_ref[:, jp * 2 * MLA_V:(jp + 1) * 2 * MLA_V]
        for j in range(2 * jp, 2 * jp + 2):
            qh = q_ref[:, j * MLA_PAD:(j + 1) * MLA_PAD]
            kh = k_ref[:, j * MLA_PAD:(j + 1) * MLA_PAD]
            s = _nt_dot(qh, kh)
            m = jnp.max(s, axis=-1, keepdims=True)
            p = jnp.exp2(s - m)
            l = jnp.sum(p, axis=-1, keepdims=True)
            outs.append(jnp.dot(p.astype(BF16), v2, preferred_element_type=F32) / l)
        o_ref[:, jp * 2 * MLA_V:(jp + 1) * 2 * MLA_V] = jnp.where(lane < MLA_V, outs[0], outs[1]).astype(o_ref.dtype)


def _mla_attn_latent(q, k, v, n_out_rows):
    nq = SEQ // MLA_Q_TILE
    qblk = lambda b, p, i: (b * nq + i, p)
    return pl.pallas_call(
        _mla_kernel,
        grid=(BATCH, MLA_HEADS // MLA_GROUP, nq),
        in_specs=[pl.BlockSpec((MLA_Q_TILE, MLA_GROUP * MLA_PAD), qblk),
                  pl.BlockSpec((None, MLA_KEYS, MLA_GROUP * MLA_PAD), lambda b, p, i: (b, 0, p)),
                  pl.BlockSpec((None, MLA_KEYS, MLA_GROUP * MLA_V), lambda b, p, i: (b, 0, p))],
        out_specs=pl.BlockSpec((MLA_Q_TILE, MLA_GROUP * MLA_V), qblk),
        out_shape=jax.ShapeDtypeStruct((n_out_rows, MLA_HEADS * MLA_V), BF16),
        compiler_params=_params("parallel", "parallel", "parallel"),
        name="mla_attn",
    )(q, k, v)


def _mla_attn_ctx(q, k, v, o):
    qblk = lambda b, p: (LAT_TILES + b, p)
    tail = lambda b, p: (b, TILES_PER_SEQ, p)
    return pl.pallas_call(
        _mla_kernel,
        grid=(BATCH, MLA_HEADS // MLA_GROUP),
        in_specs=[pl.BlockSpec((TOK_TILE, MLA_GROUP * MLA_PAD), qblk),
                  pl.BlockSpec((None, CTX_LEN, MLA_GROUP * MLA_PAD), tail),
                  pl.BlockSpec((None, CTX_LEN, MLA_GROUP * MLA_V), tail),
                  pl.BlockSpec(memory_space=pl.ANY)],
        out_specs=pl.BlockSpec((TOK_TILE, MLA_GROUP * MLA_V), qblk),
        out_shape=jax.ShapeDtypeStruct((N_TOK, MLA_HEADS * MLA_V), BF16),
        input_output_aliases={3: 0},
        compiler_params=_params("parallel", "parallel"),
        name="mla_attn_ctx",
    )(q, k, v, o)


def _split_bf16(a):
    hi = a.astype(BF16)
    return hi, (a - hi.astype(F32)).astype(BF16)


def _outproj_kernel(n_in, *refs):
    a_refs = refs[:n_in]
    w_refs = refs[n_in:2 * n_in]
    x_ref, g1_ref, gn_ref, sh_ref, sc_ref, rw_ref, rb_ref, xo_ref, f_ref, lg_ref = refs[2 * n_in:]
    y = jnp.dot(a_refs[0][...], w_refs[0][...], preferred_element_type=F32)
    for a_ref, w_ref in zip(a_refs[1:], w_refs[1:]):
        y = y + jnp.dot(a_ref[...], w_ref[...], preferred_element_type=F32)
    xo_ref[...] = x_ref[...] + g1_ref[...] * y
    f = _modulated_norm(xo_ref, gn_ref, sh_ref, sc_ref)
    _store_token_tiles(f_ref, f)
    f_hi, f_lo = _split_bf16(f)
    w_hi, w_lo = _split_bf16(rw_ref[...])
    mm = lambda a, b: jnp.dot(a, b, preferred_element_type=F32)
    lg_ref[...] = mm(f_hi, w_hi) + (mm(f_lo, w_hi) + mm(f_hi, w_lo)) + rb_ref[...]


def _outproj(a_list, w_list, x, n_rows, mods4, gn, rw_pad, rb_pad):
    n_in = len(a_list)
    row = lambda n: pl.BlockSpec((OUT_TILE, n), lambda i: (i, 0))
    in_specs = [row(a.shape[1]) for a in a_list] + [_full(w.shape) for w in w_list]
    in_specs += [row(D_MODEL), _mod_spec(2, OUT_TILE), _full((1, D_MODEL)),
                 _mod_spec(3, OUT_TILE), _mod_spec(4, OUT_TILE), _full((D_MODEL, LANE)), _full((1, LANE))]
    return pl.pallas_call(
        functools.partial(_outproj_kernel, n_in),
        grid=(n_rows // OUT_TILE,),
        in_specs=in_specs,
        out_specs=[row(D_MODEL), pl.BlockSpec((OUT_TILE * ROW_SUB, LANE), lambda i: (i, 0)), row(LANE)],
        out_shape=[jax.ShapeDtypeStruct((n_rows, D_MODEL), F32),
                   jax.ShapeDtypeStruct((n_rows * ROW_SUB, LANE), F32),
                   jax.ShapeDtypeStruct((n_rows, LANE), F32)],
        compiler_params=_params("parallel"),
        name="outproj",
    )(*a_list, *w_list, x, mods4, gn.reshape(1, D_MODEL), mods4, mods4, rw_pad, rb_pad)


IDX_LANE = 0
GATE_LANE = TOP_K
POS_LANE = 2 * TOP_K
MASKED = -3.0e38


def _route_kernel(lg_ref, info_ref, cnt_ref, carry_ref):
    i = pl.program_id(0)

    @pl.when(i == 0)
    def _():
        carry_ref[...] = jnp.zeros_like(carry_ref)

    lane = lax.broadcasted_iota(jnp.int32, (TOK_TILE, LANE), 1).astype(F32)
    x = jnp.where(lane < N_EXPERTS, lg_ref[...], MASKED)
    vals, idxs = [], []
    for _ in range(TOP_K):
        m = jnp.max(x, axis=-1, keepdims=True)
        idx = jnp.min(jnp.where(x == m, lane, float(LANE)), axis=-1, keepdims=True)
        vals.append(m)
        idxs.append(idx)
        x = jnp.where(lane == idx, MASKED, x)
    es = [jnp.exp(v - vals[0]) for v in vals]
    denom = es[0] + es[1] + es[2] + es[3]
    onehot = jnp.zeros((TOK_TILE, LANE), F32)
    for idx in idxs:
        onehot = onehot + jnp.where(lane == idx, 1.0, 0.0)
    r = lax.broadcasted_iota(jnp.int32, (TOK_TILE, TOK_TILE), 0)
    c = lax.broadcasted_iota(jnp.int32, (TOK_TILE, TOK_TILE), 1)
    tri = jnp.where(r > c, 1.0, 0.0).astype(BF16)
    before = jnp.dot(tri, onehot.astype(BF16), preferred_element_type=F32) + carry_ref[...]
    info = jnp.zeros((TOK_TILE, LANE), F32)
    for k in range(TOP_K):
        pos = jnp.sum(jnp.where(lane == idxs[k], before, 0.0), axis=-1, keepdims=True)
        info = jnp.where(lane == IDX_LANE + k, idxs[k], info)
        info = jnp.where(lane == GATE_LANE + k, es[k] / denom, info)
        info = jnp.where(lane == POS_LANE + k, pos, info)
    info_ref[...] = info
    carry_ref[...] = carry_ref[...] + jnp.sum(onehot, axis=0, keepdims=True)
    cnt_ref[...] = carry_ref[...]


def _route(logits, n_rows):
    return pl.pallas_call(
        _route_kernel,
        grid=(n_rows // TOK_TILE,),
        in_specs=[pl.BlockSpec((TOK_TILE, LANE), lambda i: (i, 0))],
        out_specs=[pl.BlockSpec((TOK_TILE, LANE), lambda i: (i, 0)), _full((1, LANE))],
        out_shape=[jax.ShapeDtypeStruct((n_rows, LANE), F32), jax.ShapeDtypeStruct((1, LANE), F32)],
        scratch_shapes=[pltpu.VMEM((1, LANE), F32)],
        compiler_params=_params("arbitrary"),
        name="route",
    )(logits)


ROW_SUB = D_MODEL // LANE


def _store_token_tiles(ref, val):
    n = val.shape[0]
    for s in range(ROW_SUB):
        ref[pl.ds(s, n, stride=ROW_SUB), :] = val[:, s * LANE:(s + 1) * LANE]


def _load_token_tiles(ref, n):
    return jnp.concatenate([ref[pl.ds(s, n, stride=ROW_SUB), :] for s in range(ROW_SUB)], axis=1)


def _tile_copy(src_ref, src_row, dst_ref, dst_row, sem):
    return pltpu.make_async_copy(src_ref.at[pl.ds(pl.multiple_of(src_row, ROW_SUB), ROW_SUB)],
                                 dst_ref.at[pl.ds(pl.multiple_of(dst_row, ROW_SUB), ROW_SUB)], sem)


def _dispatch_kernel(dest_ref, f_ref, xb_ref, sem):
    def issue(r, carry):
        for k in range(TOP_K):
            _tile_copy(f_ref, r * ROW_SUB, xb_ref, dest_ref[0, r * TOP_K + k], sem).start()
        return carry

    lax.fori_loop(0, TOK_TILE, issue, 0, unroll=8)
    for _ in range(TOP_K):
        pltpu.make_async_copy(f_ref, xb_ref.at[pl.ds(0, TOK_TILE * ROW_SUB)], sem).wait()


def _dispatch(dest3, f, n_rows_out):
    n_rows = dest3.shape[0] * TOK_TILE
    return pl.pallas_call(
        _dispatch_kernel,
        grid=(n_rows // TOK_TILE,),
        in_specs=[pl.BlockSpec((None, 1, TOK_TILE * TOP_K), lambda i: (i, 0, 0), memory_space=pltpu.SMEM),
                  pl.BlockSpec((TOK_TILE * ROW_SUB, LANE), lambda i: (i, 0))],
        out_specs=pl.BlockSpec(memory_space=pl.ANY),
        out_shape=jax.ShapeDtypeStruct((n_rows_out * ROW_SUB, LANE), F32),
        scratch_shapes=[pltpu.SemaphoreType.DMA(())],
        compiler_params=_params("arbitrary"),
        name="moe_dispatch",
    )(dest3, f)


def _moe_kernel(layer, be_ref, nv_ref, first_ref, nxt_ref, x_ref, wgu_hbm, bgu_ref, wd_hbm, bd_ref, o_ref,
                wgu_f32, wd_f32, wgu_s, wd_s, sems):
    i = pl.program_id(0)

    def fetch(e):
        return (pltpu.make_async_copy(wgu_hbm.at[layer, e], wgu_f32, sems.at[0]),
                pltpu.make_async_copy(wd_hbm.at[layer, e], wd_f32, sems.at[1]))

    @pl.when(i == 0)
    def _():
        for cp in fetch(be_ref[0]):
            cp.start()

    @pl.when(first_ref[i] == 1)
    def _():
        for cp in fetch(be_ref[i]):
            cp.wait()
        wgu_s[...] = wgu_f32[...].astype(BF16)
        wd_s[...] = wd_f32[...].astype(BF16)

        @pl.when(nxt_ref[i] >= 0)
        def _():
            for cp in fetch(nxt_ref[i]):
                cp.start()

    @pl.when(nv_ref[i] == 0)
    def _():
        o_ref[...] = jnp.zeros_like(o_ref)

    @pl.when(nv_ref[i] > 0)
    def _():
        row = lax.broadcasted_iota(jnp.int32, (MOE_ROWS, D_MODEL), 0)
        x = _load_token_tiles(x_ref, MOE_ROWS)
        x = jnp.where(row < nv_ref[i], x, 0.0).astype(BF16)
        gu = jnp.dot(x, wgu_s[...], preferred_element_type=F32) + bgu_ref[...]
        gate = jnp.minimum(gu[:, :D_EXPERT], SWIGLU_LIMIT)
        up = jnp.clip(gu[:, D_EXPERT:], -SWIGLU_LIMIT, SWIGLU_LIMIT)
        act = (up + 1.0) * (gate * jax.nn.sigmoid(SWIGLU_ALPHA * gate))
        _store_token_tiles(o_ref, jnp.dot(act.astype(BF16), wd_s[...], preferred_element_type=F32) + bd_ref[...])


def _moe_experts(layer, block_expert, rows_valid, first, nxt, xb, w_gu, b_gu, w_down, b_down):
    n_rows = xb.shape[0] // ROW_SUB
    bsel = lambda i, be, nv, fi, nx: (layer, be[i], 0, 0)
    rows = lambda i, be, nv, fi, nx: (i, 0)
    grid_spec = pltpu.PrefetchScalarGridSpec(
        num_scalar_prefetch=4,
        grid=(n_rows // MOE_ROWS,),
        in_specs=[
            pl.BlockSpec((MOE_ROWS * ROW_SUB, LANE), rows),
            pl.BlockSpec(memory_space=pl.ANY),
            pl.BlockSpec((None, None, 1, 2 * D_EXPERT), bsel),
            pl.BlockSpec(memory_space=pl.ANY),
            pl.BlockSpec((None, None, 1, D_MODEL), bsel),
        ],
        out_specs=pl.BlockSpec((MOE_ROWS * ROW_SUB, LANE), rows),
        scratch_shapes=[pltpu.VMEM((D_MODEL, 2 * D_EXPERT), F32),
                        pltpu.VMEM((D_EXPERT, D_MODEL), F32),
                        pltpu.VMEM((D_MODEL, 2 * D_EXPERT), BF16),
                        pltpu.VMEM((D_EXPERT, D_MODEL), BF16),
                        pltpu.SemaphoreType.DMA((2,))],
    )
    return pl.pallas_call(
        functools.partial(_moe_kernel, layer),
        grid_spec=grid_spec,
        out_shape=jax.ShapeDtypeStruct((n_rows * ROW_SUB, LANE), F32),
        compiler_params=_params("arbitrary"),
        name="moe_experts",
    )(block_expert, rows_valid, first, nxt, xb, w_gu, b_gu.reshape(DEPTH, N_EXPERTS, 1, -1), w_down,
      b_down.reshape(DEPTH, N_EXPERTS, 1, -1))


def _combine_kernel(dest_ref, x_ref, g2_ref, info_ref, yb_ref, o_ref, buf, sem):
    def issue(r, carry):
        for k in range(TOP_K):
            _tile_copy(yb_ref, dest_ref[0, r * TOP_K + k], buf.at[k], r * ROW_SUB, sem).start()
        return carry

    lax.fori_loop(0, TOK_TILE, issue, 0, unroll=8)
    for k in range(TOP_K):
        pltpu.make_async_copy(yb_ref.at[pl.ds(0, TOK_TILE * ROW_SUB)], buf.at[k], sem).wait()
    info = info_ref[...]
    y = info[:, GATE_LANE:GATE_LANE + 1] * _load_token_tiles(buf.at[0], TOK_TILE)
    for k in range(1, TOP_K):
        y = y + info[:, GATE_LANE + k:GATE_LANE + k + 1] * _load_token_tiles(buf.at[k], TOK_TILE)
    o_ref[...] = x_ref[...] + g2_ref[...] * y


def _combine(dest3, x, mods4, info, yb):
    n_rows = dest3.shape[0] * TOK_TILE
    row = lambda n: pl.BlockSpec((TOK_TILE, n), lambda i: (i, 0))
    return pl.pallas_call(
        _combine_kernel,
        grid=(n_rows // TOK_TILE,),
        in_specs=[pl.BlockSpec((None, 1, TOK_TILE * TOP_K), lambda i: (i, 0, 0), memory_space=pltpu.SMEM),
                  row(D_MODEL), _mod_spec(5, TOK_TILE), row(LANE), pl.BlockSpec(memory_space=pl.ANY)],
        out_specs=row(D_MODEL),
        out_shape=jax.ShapeDtypeStruct((n_rows, D_MODEL), F32),
        scratch_shapes=[pltpu.VMEM((TOP_K, TOK_TILE * ROW_SUB, LANE), F32), pltpu.SemaphoreType.DMA(())],
        compiler_params=_params("arbitrary"),
        name="moe_combine",
    )(dest3, x, mods4, info, yb)


def _moe_layer(layer, x_new, f, logits, n_rows, mods4, w_gu, b_gu, w_down, b_down):
    info, cnt = _route(logits, n_rows)
    idx = info[:, IDX_LANE:IDX_LANE + TOP_K].astype(jnp.int32)
    pos = info[:, POS_LANE:POS_LANE + TOP_K].astype(jnp.int32)
    counts = cnt[0, :N_EXPERTS].astype(jnp.int32)
    padded = (counts + MOE_ROWS - 1) // MOE_ROWS * MOE_ROWS
    pad_ends = jnp.cumsum(padded)
    pad_starts = pad_ends - padded
    experts = jnp.arange(N_EXPERTS, dtype=jnp.int32)
    dest = pos + jnp.sum(jnp.where(idx[:, :, None] == experts, pad_starts, 0), axis=-1)
    dest3 = (dest * ROW_SUB).reshape(n_rows // TOK_TILE, 1, TOK_TILE * TOP_K)
    n_blocks = n_rows * TOP_K // MOE_ROWS + N_EXPERTS
    block_start = jnp.arange(n_blocks, dtype=jnp.int32) * MOE_ROWS
    used = counts > 0
    last_used = jnp.max(jnp.where(used, experts, 0))
    block_expert = jnp.minimum(jnp.sum((block_start[:, None] >= pad_ends[None, :]).astype(jnp.int32), axis=1),
                               last_used)
    group_end = (pad_starts + counts)[block_expert]
    rows_valid = jnp.clip(group_end - block_start, 0, MOE_ROWS).astype(jnp.int32)
    first = jnp.concatenate([jnp.ones((1,), jnp.int32),
                             (block_expert[1:] != block_expert[:-1]).astype(jnp.int32)])
    later = jnp.where(used[None, :] & (experts[None, :] > experts[:, None]), experts[None, :], N_EXPERTS)
    next_used = jnp.min(later, axis=1)
    nxt = jnp.where(next_used < N_EXPERTS, next_used, -1)[block_expert].astype(jnp.int32)
    xb = _dispatch(dest3, f, n_blocks * MOE_ROWS)
    yb = _moe_experts(layer, block_expert, rows_valid, first, nxt, xb, w_gu, b_gu, w_down, b_down)
    return _combine(dest3, x_new, mods4, info, yb)


def _pad_lanes(v, start=0):
    return jnp.zeros((1, LANE), F32).at[0, start:start + v.shape[0]].set(v)


def kernel(x, c, ctx, c_ctx, mod_w, mod_b, norm_mix_g, norm_ffn_g, ab_w_in, ab_w_out, a_q_norm, a_k_norm, a_sink, b_dw_w, b_dw_b, b_ln_g, b_ln_b, c_w_in, c_q_a_norm, c_kv_a_norm, c_w_q_b, c_w_kv_b, c_q_norm, c_k_norm, c_kr_norm, c_w_out, router_w, router_b, exp_w_gu, exp_b_gu, exp_w_down, exp_b_down):
    cos_a, sin_a = _rope_lane_tables(HEAD_DIM, 0, LANE // HEAD_DIM)
    cos_c, sin_c = _rope_lane_tables(MLA_ROPE, MLA_NOPE, 1)
    cvec = jnp.concatenate([c, c_ctx[None, :], jnp.zeros((SEG_PAD - N_SEG, D_MODEL), F32)], axis=0)
    mods = _modulation(cvec, mod_w, mod_b)
    xt = jnp.concatenate([x.reshape(N_LAT, D_MODEL), ctx.reshape(N_CTX, D_MODEL)], axis=0)
    for layer in range(DEPTH):
        keep_ctx = layer < DEPTH - 1
        n_rows = N_TOK if keep_ctx else N_LAT
        mods4 = mods[layer].reshape(SEG_PAD, 6, 1, D_MODEL)
        i = layer // 2
        if layer % 2 == 0:
            qg2 = jnp.tile(a_q_norm[i], LANE // HEAD_DIM).reshape(1, LANE)
            kg2 = jnp.tile(a_k_norm[i], LANE // HEAD_DIM).reshape(1, LANE)
            q, k, v, z = _even_inproj(xt, norm_mix_g[layer], mods4, ab_w_in[i].astype(BF16), qg2, kg2, cos_a, sin_a)
            sink_b = jnp.broadcast_to(a_sink[i].astype(F32)[:, None], (A_Q_HEADS, LANE))
            a = _win_attn(q, k, v, sink_b)
            b = _conv_module(z, b_dw_w[i], b_dw_b[i], b_ln_g[i], b_ln_b[i])
            w_out = ab_w_out[i].astype(BF16)
            a_list, w_list = [a, b], [w_out[:DQ], w_out[DQ:]]
        else:
            w_in = c_w_in[i]
            zeros = lambda n: jnp.zeros((D_MODEL, n), F32)
            w_in_p = jnp.concatenate([w_in[:, :Q_LORA + KV_LORA], zeros(MLA_NOPE), w_in[:, Q_LORA + KV_LORA:],
                                      zeros(LANE - MLA_QK)], axis=1).astype(BF16)
            w_q = jnp.pad(c_w_q_b[i].reshape(Q_LORA, MLA_HEADS, MLA_QK),
                          ((0, 0), (0, 0), (0, MLA_PAD - MLA_QK))).reshape(Q_LORA, MLA_HEADS * MLA_PAD)
            w_kv = c_w_kv_b[i].reshape(KV_LORA, MLA_HEADS, MLA_NOPE + MLA_V)
            w_k = jnp.pad(w_kv[:, :, :MLA_NOPE], ((0, 0), (0, 0), (0, MLA_PAD - MLA_NOPE)))
            w_kv_p = jnp.concatenate([w_k.reshape(KV_LORA, -1), w_kv[:, :, MLA_NOPE:].reshape(KV_LORA, -1)], axis=1)
            q, k, v = _mla_proj(xt, norm_mix_g[layer], mods4, w_in_p, c_q_a_norm[i].reshape(1, -1),
                                c_kv_a_norm[i].reshape(1, -1), _pad_lanes(c_kr_norm[i], MLA_NOPE),
                                w_q.astype(BF16), w_kv_p.astype(BF16), _pad_lanes(c_q_norm[i]),
                                _pad_lanes(c_k_norm[i]), cos_c, sin_c)
            o = _mla_attn_latent(q, k, v, n_rows)
            if keep_ctx:
                o = _mla_attn_ctx(q, k, v, o)
            a_list, w_list = [o], [c_w_out[i].astype(BF16)]
        rw_pad = jnp.pad(router_w[layer], ((0, 0), (0, LANE - N_EXPERTS)))
        rb_pad = jnp.pad(router_b[layer], (0, LANE - N_EXPERTS)).reshape(1, LANE)
        x_new, f, logits = _outproj(a_list, w_list, xt, n_rows, mods4, norm_ffn_g[layer], rw_pad, rb_pad)
        xt = _moe_layer(layer, x_new, f, logits, n_rows, mods4, exp_w_gu, exp_b_gu, exp_w_down, exp_b_down)
    return xt.reshape(BATCH, SEQ, D_MODEL)
```

```python
import functools

import jax
import jax.numpy as jnp
import numpy as np
from jax import lax
from jax.experimental import pallas as pl
from jax.experimental.pallas import tpu as pltpu

D_MODEL = 1024
BATCH = 4
SEQ = 4096
DEPTH = 4
GRID_W = 64
CTX_LEN = 256
A_Q_HEADS = 8
A_KV_HEADS = 2
HEAD_DIM = 64
WINDOW = 128
CONV_CH = 512
CONV_WIDTH = 31
MLA_HEADS = 16
MLA_NOPE = 64
MLA_ROPE = 32
MLA_V = 64
MLA_QK = MLA_NOPE + MLA_ROPE
Q_LORA = 768
KV_LORA = 256
N_EXPERTS = 32
TOP_K = 4
D_EXPERT = 1024
SWIGLU_LIMIT = 7.0
SWIGLU_ALPHA = 1.702
ROPE_THETA = 10000.0
EPS = 1e-6
NEG_INF = -1e30

LANE = 128
N_LAT = BATCH * SEQ
N_CTX = BATCH * CTX_LEN
N_TOK = N_LAT + N_CTX
N_SEG = BATCH + 1
SEG_PAD = 8
DQ = A_Q_HEADS * HEAD_DIM
DKV = A_KV_HEADS * HEAD_DIM
DKV_DUP = A_KV_HEADS * LANE
AB_IN = DQ + 2 * DKV + 2 * CONV_CH
MLA_PAD = LANE
C_IN_PAD = Q_LORA + KV_LORA + LANE
MLA_KEYS = SEQ + CTX_LEN

TOK_TILE = 256
LAT_TILES = N_LAT // TOK_TILE
TILES_PER_SEQ = SEQ // TOK_TILE
OUT_TILE = 512
MOE_ROWS = 256
CONV_HALO = 16
VMEM_LIMIT = 56 * 1024 * 1024

F32 = jnp.float32
BF16 = jnp.bfloat16


def _params(*sem):
    return pltpu.CompilerParams(dimension_semantics=sem, vmem_limit_bytes=VMEM_LIMIT)


def _seg_of_tile(i, tile):
    return jnp.minimum(i * tile // SEQ, BATCH)


def _rope_block_of_tile(i):
    return jnp.where(i < LAT_TILES, i % TILES_PER_SEQ, TILES_PER_SEQ)


def _mod_spec(chunk, tile):
    return pl.BlockSpec((None, None, 1, D_MODEL), lambda i: (_seg_of_tile(i, tile), chunk, 0, 0))


def _full(shape):
    return pl.BlockSpec(shape, lambda *_: (0,) * len(shape))


def _rms(x, n=None):
    n = x.shape[-1] if n is None else n
    return x * lax.rsqrt(jnp.sum(x * x, axis=-1, keepdims=True) * (1.0 / n) + EPS)


def _modulated_norm(x_ref, g_ref, sh_ref, sc_ref):
    h = _rms(x_ref[...]) * g_ref[...]
    return h * (1.0 + sc_ref[...]) + sh_ref[...]


def _mod_kernel(c_ref, w_ref, b_ref, o_ref):
    c = c_ref[...]
    s = c * jax.nn.sigmoid(c)
    o_ref[...] = jnp.dot(s, w_ref[...], precision=lax.Precision.HIGHEST,
                         preferred_element_type=F32) + b_ref[...]


def _modulation(cvec, mod_w, mod_b):
    tn = 1536
    return pl.pallas_call(
        _mod_kernel,
        grid=(DEPTH, 6 * D_MODEL // tn),
        in_specs=[
            pl.BlockSpec((SEG_PAD, D_MODEL), lambda l, j: (0, 0)),
            pl.BlockSpec((None, D_MODEL, tn), lambda l, j: (l, 0, j)),
            pl.BlockSpec((None, 1, tn), lambda l, j: (l, 0, j)),
        ],
        out_specs=pl.BlockSpec((None, SEG_PAD, tn), lambda l, j: (l, 0, j)),
        out_shape=jax.ShapeDtypeStruct((DEPTH, SEG_PAD, 6 * D_MODEL), F32),
        compiler_params=_params("arbitrary", "arbitrary"),
        name="modulation",
    )(cvec, mod_w, mod_b.reshape(DEPTH, 1, 6 * D_MODEL))


def _rope_lanes(t, cos, sin, half0, shift):
    partner = jnp.where(half0, pltpu.roll(t, LANE - shift, 1), pltpu.roll(t, shift, 1))
    return t * cos + partner * sin


def _rope_lane_tables(rot_dims, lane_start, repeat):
    f32 = np.float32
    rows = SEQ // GRID_W
    row = np.repeat(np.arange(rows), GRID_W).astype(f32)
    col = np.tile(np.arange(GRID_W), rows).astype(f32)
    axis_dims = rot_dims // 2
    n = axis_dims // 2
    inv = (f32(ROPE_THETA) ** (-(np.arange(n, dtype=f32) * f32(2.0)) / f32(axis_dims))).astype(f32)
    ang = np.concatenate([row[:, None] * inv, col[:, None] * inv], axis=-1)
    d = np.arange(rot_dims)
    src = (d // (2 * n)) * n + d % n
    sign = np.where((d % (2 * n)) // n == 0, -1.0, 1.0).astype(f32)
    cos_t = np.ones((SEQ + TOK_TILE, LANE), f32)
    sin_t = np.zeros((SEQ + TOK_TILE, LANE), f32)
    for r in range(repeat):
        lo = lane_start + r * rot_dims
        cos_t[:SEQ, lo:lo + rot_dims] = np.cos(ang)[:, src]
        sin_t[:SEQ, lo:lo + rot_dims] = np.sin(ang)[:, src] * sign
    return jnp.asarray(cos_t), jnp.asarray(sin_t)


def _even_inproj_kernel(x_ref, g_ref, sh_ref, sc_ref, w_ref, qg_ref, kg_ref, cos_ref, sin_ref,
                        q_ref, k_ref, v_ref, z_ref):
    h = _modulated_norm(x_ref, g_ref, sh_ref, sc_ref)
    p = jnp.dot(h.astype(BF16), w_ref[...], preferred_element_type=F32)
    lane = lax.broadcasted_iota(jnp.int32, (TOK_TILE, LANE), 1)
    lo = lane < HEAD_DIM
    half0 = (lane & (HEAD_DIM // 2 - 1)) < HEAD_DIM // 4
    cos = cos_ref[...]
    sin = sin_ref[...]

    def norm_rope(t, gvec):
        ss = t * t
        s_lo = jnp.sum(jnp.where(lo, ss, 0.0), axis=-1, keepdims=True)
        s_hi = jnp.sum(jnp.where(lo, 0.0, ss), axis=-1, keepdims=True)
        ms = jnp.where(lo, s_lo, s_hi) * (1.0 / HEAD_DIM)
        tn = t * lax.rsqrt(ms + EPS) * gvec
        return _rope_lanes(tn, cos, sin, half0, HEAD_DIM // 4)

    scale = HEAD_DIM ** -0.5
    for j in range(DQ // LANE):
        q_ref[:, j * LANE:(j + 1) * LANE] = (norm_rope(p[:, j * LANE:(j + 1) * LANE], qg_ref[...]) * scale).astype(BF16)
    def twice(t, ref):
        sw = pltpu.roll(t, HEAD_DIM, 1)
        ref[:, :LANE] = jnp.where(lo, t, sw).astype(BF16)
        ref[:, LANE:] = jnp.where(lo, sw, t).astype(BF16)

    twice(norm_rope(p[:, DQ:DQ + DKV], kg_ref[...]), k_ref)
    twice(p[:, DQ + DKV:DQ + 2 * DKV], v_ref)
    u0 = DQ + 2 * DKV
    z_ref[...] = (p[:, u0:u0 + CONV_CH] * jax.nn.sigmoid(p[:, u0 + CONV_CH:])).astype(BF16)


def _even_inproj(xt, g, mods4, w_bf, qg2, kg2, cos_t, sin_t):
    row = lambda n: pl.BlockSpec((TOK_TILE, n), lambda i: (i, 0))
    tab = pl.BlockSpec((TOK_TILE, LANE), lambda i: (_rope_block_of_tile(i), 0))
    return pl.pallas_call(
        _even_inproj_kernel,
        grid=(N_TOK // TOK_TILE,),
        in_specs=[row(D_MODEL), _full((1, D_MODEL)), _mod_spec(0, TOK_TILE), _mod_spec(1, TOK_TILE),
                  _full((D_MODEL, AB_IN)), _full((1, LANE)), _full((1, LANE)), tab, tab],
        out_specs=[row(DQ), row(DKV_DUP), row(DKV_DUP), row(CONV_CH)],
        out_shape=[jax.ShapeDtypeStruct((N_TOK, DQ), BF16),
                   jax.ShapeDtypeStruct((N_TOK, DKV_DUP), BF16),
                   jax.ShapeDtypeStruct((N_TOK, DKV_DUP), BF16),
                   jax.ShapeDtypeStruct((N_TOK, CONV_CH), BF16)],
        compiler_params=_params("parallel"),
        name="even_inproj",
    )(xt, g.reshape(1, D_MODEL), mods4, mods4, w_bf, qg2, kg2, cos_t, sin_t)


def _softmax_pv(s, sk, v):
    m = jnp.maximum(jnp.max(s, axis=-1, keepdims=True), sk)
    p = jnp.exp(s - m)
    l = jnp.sum(p, axis=-1, keepdims=True) + jnp.exp(sk - m)
    o = jnp.dot(p.astype(BF16), v, preferred_element_type=F32)
    return o / l


def _nt_dot(a, b):
    return lax.dot_general(a, b, (((1,), (1,)), ((), ())), preferred_element_type=F32)


N_WIN = SEQ // WINDOW
CTX_QBLOCKS = CTX_LEN // WINDOW


def _win_attn_kernel(q_ref, kp_ref, kc_ref, kn_ref, vp_ref, vc_ref, vn_ref, kx_ref, vx_ref,
                     sink_ref, o_ref):
    n = pl.program_id(1)
    w = WINDOW
    k_all = jnp.concatenate([kp_ref[...], kc_ref[...], kn_ref[...], kx_ref[...]], axis=0)
    v_all = jnp.concatenate([vp_ref[...], vc_ref[...], vn_ref[...], vx_ref[...]], axis=0)
    nk = 3 * w + CTX_LEN
    group = A_Q_HEADS // A_KV_HEADS
    row = lax.broadcasted_iota(jnp.int32, (group * w, nk), 0)
    qi = row & (w - 1)
    ki = lax.broadcasted_iota(jnp.int32, (group * w, nk), 1)
    rel = ki - w - qi
    kpos = (n - 1) * w + ki
    kpos_end = jnp.where(n < N_WIN, SEQ, -1)
    valid = (ki >= 3 * w) | ((jnp.abs(rel) <= w) & (kpos >= 0) & (kpos < kpos_end))
    lo = lax.broadcasted_iota(jnp.int32, (w, LANE), 1) < HEAD_DIM
    head_of_row = lax.broadcasted_iota(jnp.int32, (group * w, 1), 0) // w
    tiles_per_kv = group * HEAD_DIM // LANE
    for g in range(A_KV_HEADS):
        kk = k_all[:, g * LANE:(g + 1) * LANE]
        vv = v_all[:, g * LANE:(g + 1) * LANE]
        parts = []
        for j in range(g * tiles_per_kv, (g + 1) * tiles_per_kv):
            qt = q_ref[:, j * LANE:(j + 1) * LANE]
            zero = jnp.zeros_like(qt)
            parts += [jnp.where(lo, qt, zero), jnp.where(lo, zero, qt)]
        qs = jnp.concatenate(parts, axis=0)
        sk = sink_ref[g * group:g * group + 1, 0:1]
        for hh in range(1, group):
            sk = jnp.where(head_of_row == hh, sink_ref[g * group + hh:g * group + hh + 1, 0:1], sk)
        s = jnp.where(valid, _nt_dot(qs, kk), NEG_INF)
        o = _softmax_pv(s, sk, vv)
        for t in range(tiles_per_kv):
            j = g * tiles_per_kv + t
            o_ref[:, j * LANE:(j + 1) * LANE] = jnp.where(
                lo, o[2 * t * w:(2 * t + 1) * w], o[(2 * t + 1) * w:(2 * t + 2) * w]).astype(o_ref.dtype)


def _win_attn(q, k, v, sink_b):
    def qblk(b, n):
        return (jnp.where(n < N_WIN, b * N_WIN + n, BATCH * N_WIN + b * CTX_QBLOCKS + (n - N_WIN)), 0)

    def local(off):
        return lambda b, n: (b * N_WIN + jnp.clip(n + off, 0, N_WIN - 1), 0)

    kv_spec = lambda f: pl.BlockSpec((WINDOW, DKV_DUP), f)
    ctx_spec = pl.BlockSpec((CTX_LEN, DKV_DUP), lambda b, n: (LAT_TILES + b, 0))
    return pl.pallas_call(
        _win_attn_kernel,
        grid=(BATCH, N_WIN + CTX_QBLOCKS),
        in_specs=[pl.BlockSpec((WINDOW, DQ), qblk),
                  kv_spec(local(-1)), kv_spec(local(0)), kv_spec(local(1)),
                  kv_spec(local(-1)), kv_spec(local(0)), kv_spec(local(1)),
                  ctx_spec, ctx_spec, _full((A_Q_HEADS, LANE))],
        out_specs=pl.BlockSpec((WINDOW, DQ), qblk),
        out_shape=jax.ShapeDtypeStruct((N_TOK, DQ), BF16),
        compiler_params=_params("parallel", "parallel"),
        name="win_attn",
    )(q, k, k, k, v, v, v, k, v, sink_b)


F32_SUBLANES = 8


def _conv_kernel(zp_ref, zc_ref, zn_ref, w_ref, b_ref, g_ref, bb_ref, o_ref, pad_ref, shift_ref):
    i = pl.program_id(0)
    tl = TOK_TILE
    h = CONV_HALO
    pos = i % TILES_PER_SEQ
    first = (i >= LAT_TILES) | (pos == 0)
    last = (i >= LAT_TILES) | (pos == TILES_PER_SEQ - 1)
    pad_ref[0:h, :] = jnp.where(first, 0.0, zp_ref[...].astype(F32))
    pad_ref[h:h + tl, :] = zc_ref[...].astype(F32)
    pad_ref[h + tl:2 * h + tl, :] = jnp.where(last, 0.0, zn_ref[...].astype(F32))
    span = tl + 2 * h - F32_SUBLANES
    for r in range(1, F32_SUBLANES):
        shift_ref[r - 1] = pad_ref[r:r + span, :]

    def tap(j):
        start = h - CONV_WIDTH // 2 + j
        r = start % F32_SUBLANES
        src = pad_ref if r == 0 else shift_ref.at[r - 1]
        return src[start - r:start - r + tl, :] * w_ref[j:j + 1, :]

    acc = tap(0)
    for j in range(1, CONV_WIDTH):
        acc = acc + tap(j)
    z = acc + b_ref[...]
    mu = jnp.mean(z, axis=-1, keepdims=True)
    zc = z - mu
    y = zc * lax.rsqrt(jnp.mean(zc * zc, axis=-1, keepdims=True) + EPS)
    y = y * g_ref[...] + bb_ref[...]
    o_ref[...] = (y * jax.nn.sigmoid(y)).astype(o_ref.dtype)


def _conv_module(z, dw_w, dw_b, ln_g, ln_b):
    r = TOK_TILE // CONV_HALO
    n_halo = N_TOK // CONV_HALO
    halo = lambda f: pl.BlockSpec((CONV_HALO, CONV_CH), f)
    vec = _full((1, CONV_CH))
    return pl.pallas_call(
        _conv_kernel,
        grid=(N_TOK // TOK_TILE,),
        in_specs=[halo(lambda i: (jnp.maximum(i * r - 1, 0), 0)),
                  pl.BlockSpec((TOK_TILE, CONV_CH), lambda i: (i, 0)),
                  halo(lambda i: (jnp.minimum((i + 1) * r, n_halo - 1), 0)),
                  _full((CONV_WIDTH, CONV_CH)), vec, vec, vec],
        out_specs=pl.BlockSpec((TOK_TILE, CONV_CH), lambda i: (i, 0)),
        out_shape=jax.ShapeDtypeStruct((N_TOK, CONV_CH), BF16),
        scratch_shapes=[pltpu.VMEM((TOK_TILE + 2 * CONV_HALO, CONV_CH), F32),
                        pltpu.VMEM((F32_SUBLANES - 1, TOK_TILE + 2 * CONV_HALO - F32_SUBLANES, CONV_CH), F32)],
        compiler_params=_params("parallel"),
        name="conv_module",
    )(z, z, z, dw_w, dw_b.reshape(1, -1), ln_g.reshape(1, -1), ln_b.reshape(1, -1))


def _mla_proj_kernel(x_ref, g_ref, sh_ref, sc_ref, win_ref, qag_ref, kvag_ref, krg_ref, wq_ref, wkv_ref,
                     qg_ref, kg_ref, cos_ref, sin_ref, q_ref, k_ref, v_ref):
    h = _modulated_norm(x_ref, g_ref, sh_ref, sc_ref)
    p = jnp.dot(h.astype(BF16), win_ref[...], preferred_element_type=F32)
    lane = lax.broadcasted_iota(jnp.int32, (TOK_TILE, LANE), 1)
    half0 = (lane & (MLA_ROPE // 2 - 1)) < MLA_ROPE // 4
    cos = cos_ref[...]
    sin = sin_ref[...]
    rope = lambda t: _rope_lanes(t, cos, sin, half0, MLA_ROPE // 4)

    cq = _rms(p[:, :Q_LORA]) * qag_ref[...]
    ckv = _rms(p[:, Q_LORA:Q_LORA + KV_LORA]) * kvag_ref[...]
    kr = rope(_rms(p[:, Q_LORA + KV_LORA:], MLA_ROPE) * krg_ref[...])

    q = jnp.dot(cq.astype(BF16), wq_ref[...], preferred_element_type=F32)
    scale = float(MLA_QK ** -0.5 * np.log2(np.e))
    for hd in range(MLA_HEADS):
        qh = _rms(q[:, hd * MLA_PAD:(hd + 1) * MLA_PAD], MLA_QK) * qg_ref[...]
        q_ref[:, hd * MLA_PAD:(hd + 1) * MLA_PAD] = (rope(qh) * scale).astype(BF16)

    kv = jnp.dot(ckv.astype(BF16), wkv_ref[...], preferred_element_type=F32)
    for hd in range(MLA_HEADS):
        kh = _rms(kv[:, hd * MLA_PAD:(hd + 1) * MLA_PAD], MLA_NOPE) * kg_ref[...]
        k_ref[:, hd * MLA_PAD:(hd + 1) * MLA_PAD] = (kh + kr).astype(BF16)
    v_ref[...] = kv[:, MLA_HEADS * MLA_PAD:].astype(BF16)


def _mla_proj(xt, g, mods4, w_in, qa_g, kva_g, kr_g, w_q, w_kv, q_g, k_g, cos_t, sin_t):
    tab = pl.BlockSpec((TOK_TILE, LANE), lambda i: (_rope_block_of_tile(i), 0))

    def key_blk(i):
        return (jnp.where(i < LAT_TILES, i // TILES_PER_SEQ, i - LAT_TILES), _rope_block_of_tile(i), 0)

    kv_n = MLA_HEADS * (MLA_PAD + MLA_V)
    return pl.pallas_call(
        _mla_proj_kernel,
        grid=(N_TOK // TOK_TILE,),
        in_specs=[pl.BlockSpec((TOK_TILE, D_MODEL), lambda i: (i, 0)), _full((1, D_MODEL)),
                  _mod_spec(0, TOK_TILE), _mod_spec(1, TOK_TILE),
                  _full((D_MODEL, C_IN_PAD)), _full((1, Q_LORA)), _full((1, KV_LORA)), _full((1, LANE)),
                  _full((Q_LORA, MLA_HEADS * MLA_PAD)), _full((KV_LORA, kv_n)),
                  _full((1, LANE)), _full((1, LANE)), tab, tab],
        out_specs=[pl.BlockSpec((TOK_TILE, MLA_HEADS * MLA_PAD), lambda i: (i, 0)),
                   pl.BlockSpec((None, TOK_TILE, MLA_HEADS * MLA_PAD), key_blk),
                   pl.BlockSpec((None, TOK_TILE, MLA_HEADS * MLA_V), key_blk)],
        out_shape=[jax.ShapeDtypeStruct((N_TOK, MLA_HEADS * MLA_PAD), BF16),
                   jax.ShapeDtypeStruct((BATCH, MLA_KEYS, MLA_HEADS * MLA_PAD), BF16),
                   jax.ShapeDtypeStruct((BATCH, MLA_KEYS, MLA_HEADS * MLA_V), BF16)],
        compiler_params=_params("parallel"),
        name="mla_proj",
    )(xt, g.reshape(1, D_MODEL), mods4, mods4, w_in, qa_g, kva_g, kr_g, w_q, w_kv, q_g, k_g, cos_t, sin_t)


MLA_GROUP = 8
MLA_Q_TILE = 256


def _mla_kernel(q_ref, k_ref, v_ref, *rest):
    o_ref = rest[-1]
    lane = lax.broadcasted_iota(jnp.int32, (q_ref.shape[0], 2 * MLA_V), 1)
    for jp in range(MLA_GROUP // 2):
        outs = []
        v2 = v_ref[:, jp * 2 * MLA_V:(jp + 1) * 2 * MLA_V]
        for j in range(2 * jp, 2 * jp + 2):
            qh = q_ref[:, j * MLA_PAD:(j + 1) * MLA_PAD]
            kh = k_ref[:, j * MLA_PAD:(j + 1) * MLA_PAD]
            s = _nt_dot(qh, kh)
            m = jnp.max(s, axis=-1, keepdims=True)
            p = jnp.exp2(s - m)
            l = jnp.sum(p, axis=-1, keepdims=True)
            outs.append(jnp.dot(p.astype(BF16), v2, preferred_element_type=F32) / l)
        o_ref[:, jp * 2 * MLA_V:(jp + 1) * 2 * MLA_V] = jnp.where(lane < MLA_V, outs[0], outs[1]).astype(o_ref.dtype)


def _mla_attn_latent(q, k, v, n_out_rows):
    nq = SEQ // MLA_Q_TILE
    qblk = lambda b, p, i: (b * nq + i, p)
    return pl.pallas_call(
        _mla_kernel,
        grid=(BATCH, MLA_HEADS // MLA_GROUP, nq),
        in_specs=[pl.BlockSpec((MLA_Q_TILE, MLA_GROUP * MLA_PAD), qblk),
                  pl.BlockSpec((None, MLA_KEYS, MLA_GROUP * MLA_PAD), lambda b, p, i: (b, 0, p)),
                  pl.BlockSpec((None, MLA_KEYS, MLA_GROUP * MLA_V), lambda b, p, i: (b, 0, p))],
        out_specs=pl.BlockSpec((MLA_Q_TILE, MLA_GROUP * MLA_V), qblk),
        out_shape=jax.ShapeDtypeStruct((n_out_rows, MLA_HEADS * MLA_V), BF16),
        compiler_params=_params("parallel", "parallel", "parallel"),
        name="mla_attn",
    )(q, k, v)


def _mla_attn_ctx(q, k, v, o):
    qblk = lambda b, p: (LAT_TILES + b, p)
    tail = lambda b, p: (b, TILES_PER_SEQ, p)
    return pl.pallas_call(
        _mla_kernel,
        grid=(BATCH, MLA_HEADS // MLA_GROUP),
        in_specs=[pl.BlockSpec((TOK_TILE, MLA_GROUP * MLA_PAD), qblk),
                  pl.BlockSpec((None, CTX_LEN, MLA_GROUP * MLA_PAD), tail),
                  pl.BlockSpec((None, CTX_LEN, MLA_GROUP * MLA_V), tail),
                  pl.BlockSpec(memory_space=pl.ANY)],
        out_specs=pl.BlockSpec((TOK_TILE, MLA_GROUP * MLA_V), qblk),
        out_shape=jax.ShapeDtypeStruct((N_TOK, MLA_HEADS * MLA_V), BF16),
        input_output_aliases={3: 0},
        compiler_params=_params("parallel", "parallel"),
        name="mla_attn_ctx",
    )(q, k, v, o)


def _split_bf16(a):
    hi = a.astype(BF16)
    return hi, (a - hi.astype(F32)).astype(BF16)


def _outproj_kernel(n_in, *refs):
    a_refs = refs[:n_in]
    w_refs = refs[n_in:2 * n_in]
    x_ref, g1_ref, gn_ref, sh_ref, sc_ref, rw_ref, rb_ref, xo_ref, f_ref, lg_ref = refs[2 * n_in:]
    y = jnp.dot(a_refs[0][...], w_refs[0][...], preferred_element_type=F32)
    for a_ref, w_ref in zip(a_refs[1:], w_refs[1:]):
        y = y + jnp.dot(a_ref[...], w_ref[...], preferred_element_type=F32)
    xo_ref[...] = x_ref[...] + g1_ref[...] * y
    f = _modulated_norm(xo_ref, gn_ref, sh_ref, sc_ref)
    _store_token_tiles(f_ref, f)
    f_hi, f_lo = _split_bf16(f)
    w_hi, w_lo = _split_bf16(rw_ref[...])
    mm = lambda a, b: jnp.dot(a, b, preferred_element_type=F32)
    lg_ref[...] = mm(f_hi, w_hi) + (mm(f_lo, w_hi) + mm(f_hi, w_lo)) + rb_ref[...]


def _outproj(a_list, w_list, x, n_rows, mods4, gn, rw_pad, rb_pad):
    n_in = len(a_list)
    row = lambda n: pl.BlockSpec((OUT_TILE, n), lambda i: (i, 0))
    in_specs = [row(a.shape[1]) for a in a_list] + [_full(w.shape) for w in w_list]
    in_specs += [row(D_MODEL), _mod_spec(2, OUT_TILE), _full((1, D_MODEL)),
                 _mod_spec(3, OUT_TILE), _mod_spec(4, OUT_TILE), _full((D_MODEL, LANE)), _full((1, LANE))]
    return pl.pallas_call(
        functools.partial(_outproj_kernel, n_in),
        grid=(n_rows // OUT_TILE,),
        in_specs=in_specs,
        out_specs=[row(D_MODEL), pl.BlockSpec((OUT_TILE * ROW_SUB, LANE), lambda i: (i, 0)), row(LANE)],
        out_shape=[jax.ShapeDtypeStruct((n_rows, D_MODEL), F32),
                   jax.ShapeDtypeStruct((n_rows * ROW_SUB, LANE), F32),
                   jax.ShapeDtypeStruct((n_rows, LANE), F32)],
        compiler_params=_params("parallel"),
        name="outproj",
    )(*a_list, *w_list, x, mods4, gn.reshape(1, D_MODEL), mods4, mods4, rw_pad, rb_pad)


IDX_LANE = 0
GATE_LANE = TOP_K
POS_LANE = 2 * TOP_K
MASKED = -3.0e38


def _route_kernel(lg_ref, info_ref, cnt_ref, carry_ref):
    i = pl.program_id(0)

    @pl.when(i == 0)
    def _():
        carry_ref[...] = jnp.zeros_like(carry_ref)

    lane = lax.broadcasted_iota(jnp.int32, (TOK_TILE, LANE), 1).astype(F32)
    x = jnp.where(lane < N_EXPERTS, lg_ref[...], MASKED)
    vals, idxs = [], []
    for _ in range(TOP_K):
        m = jnp.max(x, axis=-1, keepdims=True)
        idx = jnp.min(jnp.where(x == m, lane, float(LANE)), axis=-1, keepdims=True)
        vals.append(m)
        idxs.append(idx)
        x = jnp.where(lane == idx, MASKED, x)
    es = [jnp.exp(v - vals[0]) for v in vals]
    denom = es[0] + es[1] + es[2] + es[3]
    onehot = jnp.zeros((TOK_TILE, LANE), F32)
    for idx in idxs:
        onehot = onehot + jnp.where(lane == idx, 1.0, 0.0)
    r = lax.broadcasted_iota(jnp.int32, (TOK_TILE, TOK_TILE), 0)
    c = lax.broadcasted_iota(jnp.int32, (TOK_TILE, TOK_TILE), 1)
    tri = jnp.where(r > c, 1.0, 0.0).astype(BF16)
    before = jnp.dot(tri, onehot.astype(BF16), preferred_element_type=F32) + carry_ref[...]
    info = jnp.zeros((TOK_TILE, LANE), F32)
    for k in range(TOP_K):
        pos = jnp.sum(jnp.where(lane == idxs[k], before, 0.0), axis=-1, keepdims=True)
        info = jnp.where(lane == IDX_LANE + k, idxs[k], info)
        info = jnp.where(lane == GATE_LANE + k, es[k] / denom, info)
        info = jnp.where(lane == POS_LANE + k, pos, info)
    info_ref[...] = info
    carry_ref[...] = carry_ref[...] + jnp.sum(onehot, axis=0, keepdims=True)
    cnt_ref[...] = carry_ref[...]


def _route(logits, n_rows):
    return pl.pallas_call(
        _route_kernel,
        grid=(n_rows // TOK_TILE,),
        in_specs=[pl.BlockSpec((TOK_TILE, LANE), lambda i: (i, 0))],
        out_specs=[pl.BlockSpec((TOK_TILE, LANE), lambda i: (i, 0)), _full((1, LANE))],
        out_shape=[jax.ShapeDtypeStruct((n_rows, LANE), F32), jax.ShapeDtypeStruct((1, LANE), F32)],
        scratch_shapes=[pltpu.VMEM((1, LANE), F32)],
        compiler_params=_params("arbitrary"),
        name="route",
    )(logits)


ROW_SUB = D_MODEL // LANE


def _store_token_tiles(ref, val):
    n = val.shape[0]
    for s in range(ROW_SUB):
        ref[pl.ds(s, n, stride=ROW_SUB), :] = val[:, s * LANE:(s + 1) * LANE]


def _load_token_tiles(ref, n):
    return jnp.concatenate([ref[pl.ds(s, n, stride=ROW_SUB), :] for s in range(ROW_SUB)], axis=1)


def _tile_copy(src_ref, src_row, dst_ref, dst_row, sem):
    return pltpu.make_async_copy(src_ref.at[pl.ds(pl.multiple_of(src_row, ROW_SUB), ROW_SUB)],
                                 dst_ref.at[pl.ds(pl.multiple_of(dst_row, ROW_SUB), ROW_SUB)], sem)


TOP_K_SHIFT = TOP_K.bit_length() - 1
assert 1 << TOP_K_SHIFT == TOP_K


def _invert_kernel(n_tok, dest_ref, src_ref, dst_ref):
    tok0 = pl.program_id(0) * TOK_TILE

    def body(t, carry):
        for k in range(TOP_K):
            d = dest_ref[0, t * TOP_K + k]
            src_ref[d] = (tok0 + t) * ROW_SUB
            dst_ref[d] = (k * n_tok + tok0 + t) * ROW_SUB
        return carry

    lax.fori_loop(0, TOK_TILE, body, 0, unroll=4)


def _invert(dest3, n_tok, n_rows_out):
    full = pl.BlockSpec((n_rows_out,), lambda i: (0,), memory_space=pltpu.SMEM)
    out = jax.ShapeDtypeStruct((n_rows_out,), jnp.int32)
    return pl.pallas_call(
        functools.partial(_invert_kernel, n_tok),
        grid=(dest3.shape[0],),
        in_specs=[pl.BlockSpec((None, 1, TOK_TILE * TOP_K), lambda i: (i, 0, 0), memory_space=pltpu.SMEM)],
        out_specs=[full, full],
        out_shape=[out, out],
        compiler_params=_params("arbitrary"),
        name="moe_invert",
    )(dest3)


def _moe_kernel(layer, n_tok, be_ref, nv_ref, first_ref, nxt_ref, inv_ref, invn_ref, invs_ref, f_hbm, wgu_hbm,
                bgu_ref, wd_hbm, bd_ref, ys_hbm, wgu_f32, wd_f32, wgu_s, wd_s, xbuf, obuf, sems, sem_g, sem_s):
    i = pl.program_id(0)
    nb = pl.num_programs(0)
    slot = i % 2
    block_rows = MOE_ROWS * ROW_SUB

    def issue_gather(blk_inv_ref, n_valid, sl):
        def body(r, carry):
            src = jnp.where(r < n_valid, blk_inv_ref[0, r], 0)
            _tile_copy(f_hbm, src, xbuf.at[sl], r * ROW_SUB, sem_g.at[sl]).start()
            return carry

        lax.fori_loop(0, MOE_ROWS, body, 0, unroll=8)

    def issue_scatter(n_valid, sl):
        dump = (n_tok * TOP_K + sl * MOE_ROWS) * ROW_SUB

        def body(r, carry):
            dst = jnp.where(r < n_valid, invs_ref[0, r], dump + r * ROW_SUB)
            _tile_copy(obuf.at[sl], r * ROW_SUB, ys_hbm, dst, sem_s.at[sl]).start()
            return carry

        lax.fori_loop(0, MOE_ROWS, body, 0, unroll=8)

    def wait_gather(sl):
        pltpu.make_async_copy(f_hbm.at[pl.ds(0, block_rows)], xbuf.at[sl], sem_g.at[sl]).wait()

    def wait_scatter(sl):
        pltpu.make_async_copy(obuf.at[sl], ys_hbm.at[pl.ds(0, block_rows)], sem_s.at[sl]).wait()

    def fetch(e):
        return (pltpu.make_async_copy(wgu_hbm.at[layer, e], wgu_f32, sems.at[0]),
                pltpu.make_async_copy(wd_hbm.at[layer, e], wd_f32, sems.at[1]))

    @pl.when((i == 0) & (nv_ref[0] > 0))
    def _():
        issue_gather(inv_ref, nv_ref[0], 0)

    nv_next = nv_ref[jnp.minimum(i + 1, nb - 1)]

    @pl.when((i + 1 < nb) & (nv_next > 0))
    def _():
        issue_gather(invn_ref, nv_next, 1 - slot)

    @pl.when(i == 0)
    def _():
        for cp in fetch(be_ref[0]):
            cp.start()

    @pl.when(first_ref[i] == 1)
    def _():
        for cp in fetch(be_ref[i]):
            cp.wait()
        wgu_s[...] = wgu_f32[...].astype(BF16)
        wd_s[...] = wd_f32[...].astype(BF16)

        @pl.when(nxt_ref[i] >= 0)
        def _():
            for cp in fetch(nxt_ref[i]):
                cp.start()

    @pl.when((i >= 2) & (nv_ref[jnp.maximum(i - 2, 0)] > 0))
    def _():
        wait_scatter(slot)

    @pl.when(nv_ref[i] > 0)
    def _():
        wait_gather(slot)
        row = lax.broadcasted_iota(jnp.int32, (MOE_ROWS, D_MODEL), 0)
        x = _load_token_tiles(xbuf.at[slot], MOE_ROWS)
        x = jnp.where(row < nv_ref[i], x, 0.0).astype(BF16)
        gu = jnp.dot(x, wgu_s[...], preferred_element_type=F32) + bgu_ref[...]
        gate = jnp.minimum(gu[:, :D_EXPERT], SWIGLU_LIMIT)
        up = jnp.clip(gu[:, D_EXPERT:], -SWIGLU_LIMIT, SWIGLU_LIMIT)
        act = (up + 1.0) * (gate * jax.nn.sigmoid(SWIGLU_ALPHA * gate))
        _store_token_tiles(obuf.at[slot],
                           jnp.dot(act.astype(BF16), wd_s[...], preferred_element_type=F32) + bd_ref[...])
        issue_scatter(nv_ref[i], slot)

    @pl.when(i == nb - 1)
    def _():
        @pl.when((nb >= 2) & (nv_ref[jnp.maximum(i - 1, 0)] > 0))
        def _():
            wait_scatter(1 - slot)

        @pl.when(nv_ref[i] > 0)
        def _():
            wait_scatter(slot)


def _moe_experts(layer, n_tok, block_expert, rows_valid, first, nxt, src3, dst3, f, w_gu, b_gu, w_down, b_down):
    n_blocks = src3.shape[0]
    bsel = lambda i, be, nv, fi, nx: (layer, be[i], 0, 0)
    inv_spec = lambda f_idx: pl.BlockSpec((None, 1, MOE_ROWS), f_idx, memory_space=pltpu.SMEM)
    grid_spec = pltpu.PrefetchScalarGridSpec(
        num_scalar_prefetch=4,
        grid=(n_blocks,),
        in_specs=[
            inv_spec(lambda i, be, nv, fi, nx: (i, 0, 0)),
            inv_spec(lambda i, be, nv, fi, nx: (jnp.minimum(i + 1, n_blocks - 1), 0, 0)),
            inv_spec(lambda i, be, nv, fi, nx: (i, 0, 0)),
            pl.BlockSpec(memory_space=pl.ANY),
            pl.BlockSpec(memory_space=pl.ANY),
            pl.BlockSpec((None, None, 1, 2 * D_EXPERT), bsel),
            pl.BlockSpec(memory_space=pl.ANY),
            pl.BlockSpec((None, None, 1, D_MODEL), bsel),
        ],
        out_specs=pl.BlockSpec(memory_space=pl.ANY),
        scratch_shapes=[pltpu.VMEM((D_MODEL, 2 * D_EXPERT), F32),
                        pltpu.VMEM((D_EXPERT, D_MODEL), F32),
                        pltpu.VMEM((D_MODEL, 2 * D_EXPERT), BF16),
                        pltpu.VMEM((D_EXPERT, D_MODEL), BF16),
                        pltpu.VMEM((2, MOE_ROWS * ROW_SUB, LANE), F32),
                        pltpu.VMEM((2, MOE_ROWS * ROW_SUB, LANE), F32),
                        pltpu.SemaphoreType.DMA((2,)),
                        pltpu.SemaphoreType.DMA((2,)),
                        pltpu.SemaphoreType.DMA((2,))],
    )
    n_slots = n_tok * TOP_K + 2 * MOE_ROWS
    return pl.pallas_call(
        functools.partial(_moe_kernel, layer, n_tok),
        grid_spec=grid_spec,
        out_shape=jax.ShapeDtypeStruct((n_slots * ROW_SUB, LANE), F32),
        compiler_params=_params("arbitrary"),
        name="moe_experts",
    )(block_expert, rows_valid, first, nxt, src3, src3, dst3, f, w_gu, b_gu.reshape(DEPTH, N_EXPERTS, 1, -1), w_down,
      b_down.reshape(DEPTH, N_EXPERTS, 1, -1))


def _combine_kernel(x_ref, g2_ref, info_ref, *refs):
    ys_refs, o_ref, gate_ref = refs[:TOP_K], refs[TOP_K], refs[TOP_K + 1]
    info = info_ref[...]
    for k in range(TOP_K):
        gate_ref[k] = jnp.broadcast_to(info[:, GATE_LANE + k:GATE_LANE + k + 1], (TOK_TILE, LANE))
    for s in range(ROW_SUB):
        y = None
        for k in range(TOP_K):
            term = gate_ref[k] * ys_refs[k][pl.ds(s, TOK_TILE, stride=ROW_SUB), :]
            y = term if y is None else y + term
        sl = slice(s * LANE, (s + 1) * LANE)
        o_ref[:, sl] = x_ref[:, sl] + g2_ref[:, sl] * y


def _combine(n_rows, x, mods4, info, ys):
    row = lambda n: pl.BlockSpec((TOK_TILE, n), lambda i: (i, 0))
    tiles = n_rows // TOK_TILE
    ys_specs = [pl.BlockSpec((TOK_TILE * ROW_SUB, LANE), lambda i, k=k: (k * tiles + i, 0)) for k in range(TOP_K)]
    return pl.pallas_call(
        _combine_kernel,
        grid=(tiles,),
        in_specs=[row(D_MODEL), _mod_spec(5, TOK_TILE), row(LANE)] + ys_specs,
        out_specs=row(D_MODEL),
        out_shape=jax.ShapeDtypeStruct((n_rows, D_MODEL), F32),
        scratch_shapes=[pltpu.VMEM((TOP_K, TOK_TILE, LANE), F32)],
        compiler_params=_params("parallel"),
        name="moe_combine",
    )(x, mods4, info, *([ys] * TOP_K))


def _moe_layer(layer, x_new, f, logits, n_rows, mods4, w_gu, b_gu, w_down, b_down):
    info, cnt = _route(logits, n_rows)
    idx = info[:, IDX_LANE:IDX_LANE + TOP_K].astype(jnp.int32)
    pos = info[:, POS_LANE:POS_LANE + TOP_K].astype(jnp.int32)
    counts = cnt[0, :N_EXPERTS].astype(jnp.int32)
    padded = (counts + MOE_ROWS - 1) // MOE_ROWS * MOE_ROWS
    pad_ends = jnp.cumsum(padded)
    pad_starts = pad_ends - padded
    experts = jnp.arange(N_EXPERTS, dtype=jnp.int32)
    dest = pos + jnp.sum(jnp.where(idx[:, :, None] == experts, pad_starts, 0), axis=-1)
    dest3 = dest.reshape(n_rows // TOK_TILE, 1, TOK_TILE * TOP_K)
    n_blocks = n_rows * TOP_K // MOE_ROWS + N_EXPERTS
    block_start = jnp.arange(n_blocks, dtype=jnp.int32) * MOE_ROWS
    used = counts > 0
    last_used = jnp.max(jnp.where(used, experts, 0))
    block_expert = jnp.minimum(jnp.sum((block_start[:, None] >= pad_ends[None, :]).astype(jnp.int32), axis=1),
                               last_used)
    group_end = (pad_starts + counts)[block_expert]
    rows_valid = jnp.clip(group_end - block_start, 0, MOE_ROWS).astype(jnp.int32)
    first = jnp.concatenate([jnp.ones((1,), jnp.int32),
                             (block_expert[1:] != block_expert[:-1]).astype(jnp.int32)])
    later = jnp.where(used[None, :] & (experts[None, :] > experts[:, None]), experts[None, :], N_EXPERTS)
    next_used = jnp.min(later, axis=1)
    nxt = jnp.where(next_used < N_EXPERTS, next_used, -1)[block_expert].astype(jnp.int32)
    src, dst = _invert(dest3, n_rows, n_blocks * MOE_ROWS)
    per_block = lambda a: a.reshape(n_blocks, 1, MOE_ROWS)
    ys = _moe_experts(layer, n_rows, block_expert, rows_valid, first, nxt, per_block(src), per_block(dst), f,
                      w_gu, b_gu, w_down, b_down)
    return _combine(n_rows, x_new, mods4, info, ys)


def _pad_lanes(v, start=0):
    return jnp.zeros((1, LANE), F32).at[0, start:start + v.shape[0]].set(v)


def kernel(x, c, ctx, c_ctx, mod_w, mod_b, norm_mix_g, norm_ffn_g, ab_w_in, ab_w_out, a_q_norm, a_k_norm, a_sink, b_dw_w, b_dw_b, b_ln_g, b_ln_b, c_w_in, c_q_a_norm, c_kv_a_norm, c_w_q_b, c_w_kv_b, c_q_norm, c_k_norm, c_kr_norm, c_w_out, router_w, router_b, exp_w_gu, exp_b_gu, exp_w_down, exp_b_down):
    cos_a, sin_a = _rope_lane_tables(HEAD_DIM, 0, LANE // HEAD_DIM)
    cos_c, sin_c = _rope_lane_tables(MLA_ROPE, MLA_NOPE, 1)
    cvec = jnp.concatenate([c, c_ctx[None, :], jnp.zeros((SEG_PAD - N_SEG, D_MODEL), F32)], axis=0)
    mods = _modulation(cvec, mod_w, mod_b)
    xt = jnp.concatenate([x.reshape(N_LAT, D_MODEL), ctx.reshape(N_CTX, D_MODEL)], axis=0)
    for layer in range(DEPTH):
        keep_ctx = layer < DEPTH - 1
        n_rows = N_TOK if keep_ctx else N_LAT
        mods4 = mods[layer].reshape(SEG_PAD, 6, 1, D_MODEL)
        i = layer // 2
        if layer % 2 == 0:
            qg2 = jnp.tile(a_q_norm[i], LANE // HEAD_DIM).reshape(1, LANE)
            kg2 = jnp.tile(a_k_norm[i], LANE // HEAD_DIM).reshape(1, LANE)
            q, k, v, z = _even_inproj(xt, norm_mix_g[layer], mods4, ab_w_in[i].astype(BF16), qg2, kg2, cos_a, sin_a)
            sink_b = jnp.broadcast_to(a_sink[i].astype(F32)[:, None], (A_Q_HEADS, LANE))
            a = _win_attn(q, k, v, sink_b)
            b = _conv_module(z, b_dw_w[i], b_dw_b[i], b_ln_g[i], b_ln_b[i])
            w_out = ab_w_out[i].astype(BF16)
            a_list, w_list = [a, b], [w_out[:DQ], w_out[DQ:]]
        else:
            w_in = c_w_in[i]
            zeros = lambda n: jnp.zeros((D_MODEL, n), F32)
            w_in_p = jnp.concatenate([w_in[:, :Q_LORA + KV_LORA], zeros(MLA_NOPE), w_in[:, Q_LORA + KV_LORA:],
                                      zeros(LANE - MLA_QK)], axis=1).astype(BF16)
            w_q = jnp.pad(c_w_q_b[i].reshape(Q_LORA, MLA_HEADS, MLA_QK),
                          ((0, 0), (0, 0), (0, MLA_PAD - MLA_QK))).reshape(Q_LORA, MLA_HEADS * MLA_PAD)
            w_kv = c_w_kv_b[i].reshape(KV_LORA, MLA_HEADS, MLA_NOPE + MLA_V)
            w_k = jnp.pad(w_kv[:, :, :MLA_NOPE], ((0, 0), (0, 0), (0, MLA_PAD - MLA_NOPE)))
            w_kv_p = jnp.concatenate([w_k.reshape(KV_LORA, -1), w_kv[:, :, MLA_NOPE:].reshape(KV_LORA, -1)], axis=1)
            q, k, v = _mla_proj(xt, norm_mix_g[layer], mods4, w_in_p, c_q_a_norm[i].reshape(1, -1),
                                c_kv_a_norm[i].reshape(1, -1), _pad_lanes(c_kr_norm[i], MLA_NOPE),
                                w_q.astype(BF16), w_kv_p.astype(BF16), _pad_lanes(c_q_norm[i]),
                                _pad_lanes(c_k_norm[i]), cos_c, sin_c)
            o = _mla_attn_latent(q, k, v, n_rows)
            if keep_ctx:
                o = _mla_attn_ctx(q, k, v, o)
            a_list, w_list = [o], [c_w_out[i].astype(BF16)]
        rw_pad = jnp.pad(router_w[layer], ((0, 0), (0, LANE - N_EXPERTS)))
        rb_pad = jnp.pad(router_b[layer], (0, LANE - N_EXPERTS)).reshape(1, LANE)
        x_new, f, logits = _outproj(a_list, w_list, xt, n_rows, mods4, norm_ffn_g[layer], rw_pad, rb_pad)
        xt = _moe_layer(layer, x_new, f, logits, n_rows, mods4, exp_w_gu, exp_b_gu, exp_w_down, exp_b_down)
    return xt.reshape(BATCH, SEQ, D_MODEL)
```

```python
import functools

import jax
import jax.numpy as jnp
import numpy as np
from jax import lax
from jax.experimental import pallas as pl
from jax.experimental.pallas import tpu as pltpu

D_MODEL = 1024
BATCH = 4
SEQ = 4096
DEPTH = 4
GRID_W = 64
CTX_LEN = 256
A_Q_HEADS = 8
A_KV_HEADS = 2
HEAD_DIM = 64
WINDOW = 128
CONV_CH = 512
CONV_WIDTH = 31
MLA_HEADS = 16
MLA_NOPE = 64
MLA_ROPE = 32
MLA_V = 64
MLA_QK = MLA_NOPE + MLA_ROPE
Q_LORA = 768
KV_LORA = 256
N_EXPERTS = 32
TOP_K = 4
D_EXPERT = 1024
SWIGLU_LIMIT = 7.0
SWIGLU_ALPHA = 1.702
ROPE_THETA = 10000.0
EPS = 1e-6
NEG_INF = -1e30

LANE = 128
N_LAT = BATCH * SEQ
N_CTX = BATCH * CTX_LEN
N_TOK = N_LAT + N_CTX
N_SEG = BATCH + 1
SEG_PAD = 8
DQ = A_Q_HEADS * HEAD_DIM
DKV = A_KV_HEADS * HEAD_DIM
DKV_DUP = A_KV_HEADS * LANE
AB_IN = DQ + 2 * DKV + 2 * CONV_CH
MLA_PAD = LANE
C_IN_PAD = Q_LORA + KV_LORA + LANE
MLA_KEYS = SEQ + CTX_LEN

TOK_TILE = 256
LAT_TILES = N_LAT // TOK_TILE
TILES_PER_SEQ = SEQ // TOK_TILE
OUT_TILE = 512
MOE_ROWS = 256
CONV_HALO = 16
VMEM_LIMIT = 56 * 1024 * 1024

F32 = jnp.float32
BF16 = jnp.bfloat16


def _params(*sem):
    return pltpu.CompilerParams(dimension_semantics=sem, vmem_limit_bytes=VMEM_LIMIT)


def _seg_of_tile(i, tile):
    return jnp.minimum(i * tile // SEQ, BATCH)


def _rope_block_of_tile(i):
    return jnp.where(i < LAT_TILES, i % TILES_PER_SEQ, TILES_PER_SEQ)


def _mod_spec(chunk, tile):
    return pl.BlockSpec((None, None, 1, D_MODEL), lambda i: (_seg_of_tile(i, tile), chunk, 0, 0))


def _full(shape):
    return pl.BlockSpec(shape, lambda *_: (0,) * len(shape))


def _rms(x, n=None):
    n = x.shape[-1] if n is None else n
    return x * lax.rsqrt(jnp.sum(x * x, axis=-1, keepdims=True) * (1.0 / n) + EPS)


def _modulated_norm(x_ref, g_ref, sh_ref, sc_ref):
    h = _rms(x_ref[...]) * g_ref[...]
    return h * (1.0 + sc_ref[...]) + sh_ref[...]


def _mod_kernel(c_ref, w_ref, b_ref, o_ref):
    c = c_ref[...]
    s = c * jax.nn.sigmoid(c)
    o_ref[...] = jnp.dot(s, w_ref[...], precision=lax.Precision.HIGHEST,
                         preferred_element_type=F32) + b_ref[...]


def _modulation(cvec, mod_w, mod_b):
    tn = 1536
    return pl.pallas_call(
        _mod_kernel,
        grid=(DEPTH, 6 * D_MODEL // tn),
        in_specs=[
            pl.BlockSpec((SEG_PAD, D_MODEL), lambda l, j: (0, 0)),
            pl.BlockSpec((None, D_MODEL, tn), lambda l, j: (l, 0, j)),
            pl.BlockSpec((None, 1, tn), lambda l, j: (l, 0, j)),
        ],
        out_specs=pl.BlockSpec((None, SEG_PAD, tn), lambda l, j: (l, 0, j)),
        out_shape=jax.ShapeDtypeStruct((DEPTH, SEG_PAD, 6 * D_MODEL), F32),
        compiler_params=_params("arbitrary", "arbitrary"),
        name="modulation",
    )(cvec, mod_w, mod_b.reshape(DEPTH, 1, 6 * D_MODEL))


def _rope_lanes(t, cos, sin, half0, shift):
    partner = jnp.where(half0, pltpu.roll(t, LANE - shift, 1), pltpu.roll(t, shift, 1))
    return t * cos + partner * sin


def _rope_lane_tables(rot_dims, lane_start, repeat):
    f32 = np.float32
    rows = SEQ // GRID_W
    row = np.repeat(np.arange(rows), GRID_W).astype(f32)
    col = np.tile(np.arange(GRID_W), rows).astype(f32)
    axis_dims = rot_dims // 2
    n = axis_dims // 2
    inv = (f32(ROPE_THETA) ** (-(np.arange(n, dtype=f32) * f32(2.0)) / f32(axis_dims))).astype(f32)
    ang = np.concatenate([row[:, None] * inv, col[:, None] * inv], axis=-1)
    d = np.arange(rot_dims)
    src = (d // (2 * n)) * n + d % n
    sign = np.where((d % (2 * n)) // n == 0, -1.0, 1.0).astype(f32)
    cos_t = np.ones((SEQ + TOK_TILE, LANE), f32)
    sin_t = np.zeros((SEQ + TOK_TILE, LANE), f32)
    for r in range(repeat):
        lo = lane_start + r * rot_dims
        cos_t[:SEQ, lo:lo + rot_dims] = np.cos(ang)[:, src]
        sin_t[:SEQ, lo:lo + rot_dims] = np.sin(ang)[:, src] * sign
    return jnp.asarray(cos_t), jnp.asarray(sin_t)


def _even_inproj_kernel(x_ref, g_ref, sh_ref, sc_ref, w_ref, qg_ref, kg_ref, cos_ref, sin_ref,
                        q_ref, k_ref, v_ref, z_ref):
    h = _modulated_norm(x_ref, g_ref, sh_ref, sc_ref)
    p = jnp.dot(h.astype(BF16), w_ref[...], preferred_element_type=F32)
    lane = lax.broadcasted_iota(jnp.int32, (TOK_TILE, LANE), 1)
    lo = lane < HEAD_DIM
    half0 = (lane & (HEAD_DIM // 2 - 1)) < HEAD_DIM // 4
    cos = cos_ref[...]
    sin = sin_ref[...]

    def norm_rope(t, gvec):
        ss = t * t
        s_lo = jnp.sum(jnp.where(lo, ss, 0.0), axis=-1, keepdims=True)
        s_hi = jnp.sum(jnp.where(lo, 0.0, ss), axis=-1, keepdims=True)
        ms = jnp.where(lo, s_lo, s_hi) * (1.0 / HEAD_DIM)
        tn = t * lax.rsqrt(ms + EPS) * gvec
        return _rope_lanes(tn, cos, sin, half0, HEAD_DIM // 4)

    scale = HEAD_DIM ** -0.5
    for j in range(DQ // LANE):
        q_ref[:, j * LANE:(j + 1) * LANE] = (norm_rope(p[:, j * LANE:(j + 1) * LANE], qg_ref[...]) * scale).astype(BF16)

    def twice(t, ref):
        sw = pltpu.roll(t, HEAD_DIM, 1)
        ref[:, :LANE] = jnp.where(lo, t, sw).astype(BF16)
        ref[:, LANE:] = jnp.where(lo, sw, t).astype(BF16)

    twice(norm_rope(p[:, DQ:DQ + DKV], kg_ref[...]), k_ref)
    twice(p[:, DQ + DKV:DQ + 2 * DKV], v_ref)
    u0 = DQ + 2 * DKV
    z_ref[...] = (p[:, u0:u0 + CONV_CH] * jax.nn.sigmoid(p[:, u0 + CONV_CH:])).astype(BF16)


def _even_inproj(xt, g, mods4, w_bf, qg2, kg2, cos_t, sin_t):
    row = lambda n: pl.BlockSpec((TOK_TILE, n), lambda i: (i, 0))
    tab = pl.BlockSpec((TOK_TILE, LANE), lambda i: (_rope_block_of_tile(i), 0))
    return pl.pallas_call(
        _even_inproj_kernel,
        grid=(N_TOK // TOK_TILE,),
        in_specs=[row(D_MODEL), _full((1, D_MODEL)), _mod_spec(0, TOK_TILE), _mod_spec(1, TOK_TILE),
                  _full((D_MODEL, AB_IN)), _full((1, LANE)), _full((1, LANE)), tab, tab],
        out_specs=[row(DQ), row(DKV_DUP), row(DKV_DUP), row(CONV_CH)],
        out_shape=[jax.ShapeDtypeStruct((N_TOK, DQ), BF16),
                   jax.ShapeDtypeStruct((N_TOK, DKV_DUP), BF16),
                   jax.ShapeDtypeStruct((N_TOK, DKV_DUP), BF16),
                   jax.ShapeDtypeStruct((N_TOK, CONV_CH), BF16)],
        compiler_params=_params("parallel"),
        name="even_inproj",
    )(xt, g.reshape(1, D_MODEL), mods4, mods4, w_bf, qg2, kg2, cos_t, sin_t)


def _softmax_pv(s, sk, v):
    m = jnp.maximum(jnp.max(s, axis=-1, keepdims=True), sk)
    p = jnp.exp(s - m)
    l = jnp.sum(p, axis=-1, keepdims=True) + jnp.exp(sk - m)
    o = jnp.dot(p.astype(BF16), v, preferred_element_type=F32)
    return o / l


def _nt_dot(a, b):
    return lax.dot_general(a, b, (((1,), (1,)), ((), ())), preferred_element_type=F32)


N_WIN = SEQ // WINDOW
CTX_QBLOCKS = CTX_LEN // WINDOW


def _win_attn_kernel(q_ref, kp_ref, kc_ref, kn_ref, vp_ref, vc_ref, vn_ref, kx_ref, vx_ref,
                     sink_ref, o_ref):
    n = pl.program_id(1)
    w = WINDOW
    k_all = jnp.concatenate([kp_ref[...], kc_ref[...], kn_ref[...], kx_ref[...]], axis=0)
    v_all = jnp.concatenate([vp_ref[...], vc_ref[...], vn_ref[...], vx_ref[...]], axis=0)
    nk = 3 * w + CTX_LEN
    group = A_Q_HEADS // A_KV_HEADS
    row = lax.broadcasted_iota(jnp.int32, (group * w, nk), 0)
    qi = row & (w - 1)
    ki = lax.broadcasted_iota(jnp.int32, (group * w, nk), 1)
    rel = ki - w - qi
    kpos = (n - 1) * w + ki
    kpos_end = jnp.where(n < N_WIN, SEQ, -1)
    valid = (ki >= 3 * w) | ((jnp.abs(rel) <= w) & (kpos >= 0) & (kpos < kpos_end))
    lo = lax.broadcasted_iota(jnp.int32, (w, LANE), 1) < HEAD_DIM
    head_of_row = lax.broadcasted_iota(jnp.int32, (group * w, 1), 0) // w
    tiles_per_kv = group * HEAD_DIM // LANE
    for g in range(A_KV_HEADS):
        kk = k_all[:, g * LANE:(g + 1) * LANE]
        vv = v_all[:, g * LANE:(g + 1) * LANE]
        parts = []
        for j in range(g * tiles_per_kv, (g + 1) * tiles_per_kv):
            qt = q_ref[:, j * LANE:(j + 1) * LANE]
            zero = jnp.zeros_like(qt)
            parts += [jnp.where(lo, qt, zero), jnp.where(lo, zero, qt)]
        qs = jnp.concatenate(parts, axis=0)
        sk = sink_ref[g * group:g * group + 1, 0:1]
        for hh in range(1, group):
            sk = jnp.where(head_of_row == hh, sink_ref[g * group + hh:g * group + hh + 1, 0:1], sk)
        s = jnp.where(valid, _nt_dot(qs, kk), NEG_INF)
        o = _softmax_pv(s, sk, vv)
        for t in range(tiles_per_kv):
            j = g * tiles_per_kv + t
            o_ref[:, j * LANE:(j + 1) * LANE] = jnp.where(
                lo, o[2 * t * w:(2 * t + 1) * w], o[(2 * t + 1) * w:(2 * t + 2) * w]).astype(o_ref.dtype)


def _win_attn(q, k, v, sink_b):
    def qblk(b, n):
        return (jnp.where(n < N_WIN, b * N_WIN + n, BATCH * N_WIN + b * CTX_QBLOCKS + (n - N_WIN)), 0)

    def local(off):
        return lambda b, n: (b * N_WIN + jnp.clip(n + off, 0, N_WIN - 1), 0)

    kv_spec = lambda f: pl.BlockSpec((WINDOW, DKV_DUP), f)
    ctx_spec = pl.BlockSpec((CTX_LEN, DKV_DUP), lambda b, n: (LAT_TILES + b, 0))
    return pl.pallas_call(
        _win_attn_kernel,
        grid=(BATCH, N_WIN + CTX_QBLOCKS),
        in_specs=[pl.BlockSpec((WINDOW, DQ), qblk),
                  kv_spec(local(-1)), kv_spec(local(0)), kv_spec(local(1)),
                  kv_spec(local(-1)), kv_spec(local(0)), kv_spec(local(1)),
                  ctx_spec, ctx_spec, _full((A_Q_HEADS, LANE))],
        out_specs=pl.BlockSpec((WINDOW, DQ), qblk),
        out_shape=jax.ShapeDtypeStruct((N_TOK, DQ), BF16),
        compiler_params=_params("parallel", "parallel"),
        name="win_attn",
    )(q, k, k, k, v, v, v, k, v, sink_b)


F32_SUBLANES = 8


def _conv_kernel(zp_ref, zc_ref, zn_ref, w_ref, b_ref, g_ref, bb_ref, o_ref, pad_ref, shift_ref):
    i = pl.program_id(0)
    tl = TOK_TILE
    h = CONV_HALO
    pos = i % TILES_PER_SEQ
    first = (i >= LAT_TILES) | (pos == 0)
    last = (i >= LAT_TILES) | (pos == TILES_PER_SEQ - 1)
    pad_ref[0:h, :] = jnp.where(first, 0.0, zp_ref[...].astype(F32))
    pad_ref[h:h + tl, :] = zc_ref[...].astype(F32)
    pad_ref[h + tl:2 * h + tl, :] = jnp.where(last, 0.0, zn_ref[...].astype(F32))
    span = tl + 2 * h - F32_SUBLANES
    for r in range(1, F32_SUBLANES):
        shift_ref[r - 1] = pad_ref[r:r + span, :]

    def tap(j):
        start = h - CONV_WIDTH // 2 + j
        r = start % F32_SUBLANES
        src = pad_ref if r == 0 else shift_ref.at[r - 1]
        return src[start - r:start - r + tl, :] * w_ref[j:j + 1, :]

    acc = tap(0)
    for j in range(1, CONV_WIDTH):
        acc = acc + tap(j)
    z = acc + b_ref[...]
    mu = jnp.mean(z, axis=-1, keepdims=True)
    zc = z - mu
    y = zc * lax.rsqrt(jnp.mean(zc * zc, axis=-1, keepdims=True) + EPS)
    y = y * g_ref[...] + bb_ref[...]
    o_ref[...] = (y * jax.nn.sigmoid(y)).astype(o_ref.dtype)


def _conv_module(z, dw_w, dw_b, ln_g, ln_b):
    r = TOK_TILE // CONV_HALO
    n_halo = N_TOK // CONV_HALO
    halo = lambda f: pl.BlockSpec((CONV_HALO, CONV_CH), f)
    vec = _full((1, CONV_CH))
    return pl.pallas_call(
        _conv_kernel,
        grid=(N_TOK // TOK_TILE,),
        in_specs=[halo(lambda i: (jnp.maximum(i * r - 1, 0), 0)),
                  pl.BlockSpec((TOK_TILE, CONV_CH), lambda i: (i, 0)),
                  halo(lambda i: (jnp.minimum((i + 1) * r, n_halo - 1), 0)),
                  _full((CONV_WIDTH, CONV_CH)), vec, vec, vec],
        out_specs=pl.BlockSpec((TOK_TILE, CONV_CH), lambda i: (i, 0)),
        out_shape=jax.ShapeDtypeStruct((N_TOK, CONV_CH), BF16),
        scratch_shapes=[pltpu.VMEM((TOK_TILE + 2 * CONV_HALO, CONV_CH), F32),
                        pltpu.VMEM((F32_SUBLANES - 1, TOK_TILE + 2 * CONV_HALO - F32_SUBLANES, CONV_CH), F32)],
        compiler_params=_params("parallel"),
        name="conv_module",
    )(z, z, z, dw_w, dw_b.reshape(1, -1), ln_g.reshape(1, -1), ln_b.reshape(1, -1))


def _mla_proj_kernel(x_ref, g_ref, sh_ref, sc_ref, win_ref, qag_ref, kvag_ref, krg_ref, wq_ref, wkv_ref,
                     qg_ref, kg_ref, cos_ref, sin_ref, q_ref, k_ref, v_ref):
    h = _modulated_norm(x_ref, g_ref, sh_ref, sc_ref)
    p = jnp.dot(h.astype(BF16), win_ref[...], preferred_element_type=F32)
    lane = lax.broadcasted_iota(jnp.int32, (TOK_TILE, LANE), 1)
    half0 = (lane & (MLA_ROPE // 2 - 1)) < MLA_ROPE // 4
    cos = cos_ref[...]
    sin = sin_ref[...]
    rope = lambda t: _rope_lanes(t, cos, sin, half0, MLA_ROPE // 4)

    cq = _rms(p[:, :Q_LORA]) * qag_ref[...]
    ckv = _rms(p[:, Q_LORA:Q_LORA + KV_LORA]) * kvag_ref[...]
    kr = rope(_rms(p[:, Q_LORA + KV_LORA:], MLA_ROPE) * krg_ref[...])

    q = jnp.dot(cq.astype(BF16), wq_ref[...], preferred_element_type=F32)
    scale = float(MLA_QK ** -0.5 * np.log2(np.e))
    for hd in range(MLA_HEADS):
        qh = _rms(q[:, hd * MLA_PAD:(hd + 1) * MLA_PAD], MLA_QK) * qg_ref[...]
        q_ref[:, hd * MLA_PAD:(hd + 1) * MLA_PAD] = (rope(qh) * scale).astype(BF16)

    kv = jnp.dot(ckv.astype(BF16), wkv_ref[...], preferred_element_type=F32)
    for hd in range(MLA_HEADS):
        kh = _rms(kv[:, hd * MLA_PAD:(hd + 1) * MLA_PAD], MLA_NOPE) * kg_ref[...]
        k_ref[:, hd * MLA_PAD:(hd + 1) * MLA_PAD] = (kh + kr).astype(BF16)
    v_ref[...] = kv[:, MLA_HEADS * MLA_PAD:].astype(BF16)


def _mla_proj(xt, g, mods4, w_in, qa_g, kva_g, kr_g, w_q, w_kv, q_g, k_g, cos_t, sin_t):
    tab = pl.BlockSpec((TOK_TILE, LANE), lambda i: (_rope_block_of_tile(i), 0))

    def key_blk(i):
        return (jnp.where(i < LAT_TILES, i // TILES_PER_SEQ, i - LAT_TILES), _rope_block_of_tile(i), 0)

    kv_n = MLA_HEADS * (MLA_PAD + MLA_V)
    return pl.pallas_call(
        _mla_proj_kernel,
        grid=(N_TOK // TOK_TILE,),
        in_specs=[pl.BlockSpec((TOK_TILE, D_MODEL), lambda i: (i, 0)), _full((1, D_MODEL)),
                  _mod_spec(0, TOK_TILE), _mod_spec(1, TOK_TILE),
                  _full((D_MODEL, C_IN_PAD)), _full((1, Q_LORA)), _full((1, KV_LORA)), _full((1, LANE)),
                  _full((Q_LORA, MLA_HEADS * MLA_PAD)), _full((KV_LORA, kv_n)),
                  _full((1, LANE)), _full((1, LANE)), tab, tab],
        out_specs=[pl.BlockSpec((TOK_TILE, MLA_HEADS * MLA_PAD), lambda i: (i, 0)),
                   pl.BlockSpec((None, TOK_TILE, MLA_HEADS * MLA_PAD), key_blk),
                   pl.BlockSpec((None, TOK_TILE, MLA_HEADS * MLA_V), key_blk)],
        out_shape=[jax.ShapeDtypeStruct((N_TOK, MLA_HEADS * MLA_PAD), BF16),
                   jax.ShapeDtypeStruct((BATCH, MLA_KEYS, MLA_HEADS * MLA_PAD), BF16),
                   jax.ShapeDtypeStruct((BATCH, MLA_KEYS, MLA_HEADS * MLA_V), BF16)],
        compiler_params=_params("parallel"),
        name="mla_proj",
    )(xt, g.reshape(1, D_MODEL), mods4, mods4, w_in, qa_g, kva_g, kr_g, w_q, w_kv, q_g, k_g, cos_t, sin_t)


MLA_GROUP = 8
MLA_Q_TILE = 256


def _mla_kernel(q_ref, k_ref, v_ref, *rest):
    o_ref = rest[-1]
    lane = lax.broadcasted_iota(jnp.int32, (q_ref.shape[0], 2 * MLA_V), 1)
    for jp in range(MLA_GROUP // 2):
        outs = []
        v2 = v_ref[:, jp * 2 * MLA_V:(jp + 1) * 2 * MLA_V]
        for j in range(2 * jp, 2 * jp + 2):
            qh = q_ref[:, j * MLA_PAD:(j + 1) * MLA_PAD]
            kh = k_ref[:, j * MLA_PAD:(j + 1) * MLA_PAD]
            s = _nt_dot(qh, kh)
            m = jnp.max(s, axis=-1, keepdims=True)
            p = jnp.exp2(s - m)
            l = jnp.sum(p, axis=-1, keepdims=True)
            outs.append(jnp.dot(p.astype(BF16), v2, preferred_element_type=F32) / l)
        o_ref[:, jp * 2 * MLA_V:(jp + 1) * 2 * MLA_V] = jnp.where(lane < MLA_V, outs[0], outs[1]).astype(o_ref.dtype)


def _mla_attn_latent(q, k, v, n_out_rows):
    nq = SEQ // MLA_Q_TILE
    qblk = lambda b, p, i: (b * nq + i, p)
    return pl.pallas_call(
        _mla_kernel,
        grid=(BATCH, MLA_HEADS // MLA_GROUP, nq),
        in_specs=[pl.BlockSpec((MLA_Q_TILE, MLA_GROUP * MLA_PAD), qblk),
                  pl.BlockSpec((None, MLA_KEYS, MLA_GROUP * MLA_PAD), lambda b, p, i: (b, 0, p)),
                  pl.BlockSpec((None, MLA_KEYS, MLA_GROUP * MLA_V), lambda b, p, i: (b, 0, p))],
        out_specs=pl.BlockSpec((MLA_Q_TILE, MLA_GROUP * MLA_V), qblk),
        out_shape=jax.ShapeDtypeStruct((n_out_rows, MLA_HEADS * MLA_V), BF16),
        compiler_params=_params("parallel", "parallel", "parallel"),
        name="mla_attn",
    )(q, k, v)


def _mla_attn_ctx(q, k, v, o):
    qblk = lambda b, p: (LAT_TILES + b, p)
    tail = lambda b, p: (b, TILES_PER_SEQ, p)
    return pl.pallas_call(
        _mla_kernel,
        grid=(BATCH, MLA_HEADS // MLA_GROUP),
        in_specs=[pl.BlockSpec((TOK_TILE, MLA_GROUP * MLA_PAD), qblk),
                  pl.BlockSpec((None, CTX_LEN, MLA_GROUP * MLA_PAD), tail),
                  pl.BlockSpec((None, CTX_LEN, MLA_GROUP * MLA_V), tail),
                  pl.BlockSpec(memory_space=pl.ANY)],
        out_specs=pl.BlockSpec((TOK_TILE, MLA_GROUP * MLA_V), qblk),
        out_shape=jax.ShapeDtypeStruct((N_TOK, MLA_HEADS * MLA_V), BF16),
        input_output_aliases={3: 0},
        compiler_params=_params("parallel", "parallel"),
        name="mla_attn_ctx",
    )(q, k, v, o)


def _split_bf16(a):
    hi = a.astype(BF16)
    return hi, (a - hi.astype(F32)).astype(BF16)


def _outproj_kernel(n_in, *refs):
    a_refs = refs[:n_in]
    w_refs = refs[n_in:2 * n_in]
    x_ref, g1_ref, gn_ref, sh_ref, sc_ref, rw_ref, rb_ref, xo_ref, f_ref, lg_ref = refs[2 * n_in:]
    y = jnp.dot(a_refs[0][...], w_refs[0][...], preferred_element_type=F32)
    for a_ref, w_ref in zip(a_refs[1:], w_refs[1:]):
        y = y + jnp.dot(a_ref[...], w_ref[...], preferred_element_type=F32)
    xo_ref[...] = x_ref[...] + g1_ref[...] * y
    f = _modulated_norm(xo_ref, gn_ref, sh_ref, sc_ref)
    _store_token_tiles(f_ref, f)
    f_hi, f_lo = _split_bf16(f)
    w_hi, w_lo = _split_bf16(rw_ref[...])
    mm = lambda a, b: jnp.dot(a, b, preferred_element_type=F32)
    lg_ref[...] = mm(f_hi, w_hi) + (mm(f_lo, w_hi) + mm(f_hi, w_lo)) + rb_ref[...]


def _outproj(a_list, w_list, x, n_rows, mods4, gn, rw_pad, rb_pad):
    n_in = len(a_list)
    row = lambda n: pl.BlockSpec((OUT_TILE, n), lambda i: (i, 0))
    in_specs = [row(a.shape[1]) for a in a_list] + [_full(w.shape) for w in w_list]
    in_specs += [row(D_MODEL), _mod_spec(2, OUT_TILE), _full((1, D_MODEL)),
                 _mod_spec(3, OUT_TILE), _mod_spec(4, OUT_TILE), _full((D_MODEL, LANE)), _full((1, LANE))]
    return pl.pallas_call(
        functools.partial(_outproj_kernel, n_in),
        grid=(n_rows // OUT_TILE,),
        in_specs=in_specs,
        out_specs=[row(D_MODEL), pl.BlockSpec((OUT_TILE * ROW_SUB, LANE), lambda i: (i, 0)), row(LANE)],
        out_shape=[jax.ShapeDtypeStruct((n_rows, D_MODEL), F32),
                   jax.ShapeDtypeStruct((n_rows * ROW_SUB, LANE), F32),
                   jax.ShapeDtypeStruct((n_rows, LANE), F32)],
        compiler_params=_params("parallel"),
        name="outproj",
    )(*a_list, *w_list, x, mods4, gn.reshape(1, D_MODEL), mods4, mods4, rw_pad, rb_pad)


IDX_LANE = 0
GATE_LANE = TOP_K
POS_LANE = 2 * TOP_K
MASKED = -3.0e38


def _route_kernel(lg_ref, info_ref, cnt_ref, carry_ref):
    i = pl.program_id(0)

    @pl.when(i == 0)
    def _():
        carry_ref[...] = jnp.zeros_like(carry_ref)

    lane = lax.broadcasted_iota(jnp.int32, (TOK_TILE, LANE), 1).astype(F32)
    x = jnp.where(lane < N_EXPERTS, lg_ref[...], MASKED)
    vals, idxs = [], []
    for _ in range(TOP_K):
        m = jnp.max(x, axis=-1, keepdims=True)
        idx = jnp.min(jnp.where(x == m, lane, float(LANE)), axis=-1, keepdims=True)
        vals.append(m)
        idxs.append(idx)
        x = jnp.where(lane == idx, MASKED, x)
    es = [jnp.exp(v - vals[0]) for v in vals]
    denom = es[0] + es[1] + es[2] + es[3]
    onehot = jnp.zeros((TOK_TILE, LANE), F32)
    for idx in idxs:
        onehot = onehot + jnp.where(lane == idx, 1.0, 0.0)
    r = lax.broadcasted_iota(jnp.int32, (TOK_TILE, TOK_TILE), 0)
    c = lax.broadcasted_iota(jnp.int32, (TOK_TILE, TOK_TILE), 1)
    tri = jnp.where(r > c, 1.0, 0.0).astype(BF16)
    before = jnp.dot(tri, onehot.astype(BF16), preferred_element_type=F32) + carry_ref[...]
    info = jnp.zeros((TOK_TILE, LANE), F32)
    for k in range(TOP_K):
        pos = jnp.sum(jnp.where(lane == idxs[k], before, 0.0), axis=-1, keepdims=True)
        info = jnp.where(lane == IDX_LANE + k, idxs[k], info)
        info = jnp.where(lane == GATE_LANE + k, es[k] / denom, info)
        info = jnp.where(lane == POS_LANE + k, pos, info)
    info_ref[...] = info
    carry_ref[...] = carry_ref[...] + jnp.sum(onehot, axis=0, keepdims=True)
    cnt_ref[...] = carry_ref[...]


def _route(logits, n_rows):
    return pl.pallas_call(
        _route_kernel,
        grid=(n_rows // TOK_TILE,),
        in_specs=[pl.BlockSpec((TOK_TILE, LANE), lambda i: (i, 0))],
        out_specs=[pl.BlockSpec((TOK_TILE, LANE), lambda i: (i, 0)), _full((1, LANE))],
        out_shape=[jax.ShapeDtypeStruct((n_rows, LANE), F32), jax.ShapeDtypeStruct((1, LANE), F32)],
        scratch_shapes=[pltpu.VMEM((1, LANE), F32)],
        compiler_params=_params("arbitrary"),
        name="route",
    )(logits)


ROW_SUB = D_MODEL // LANE


def _store_token_tiles(ref, val):
    n = val.shape[0]
    for s in range(ROW_SUB):
        ref[pl.ds(s, n, stride=ROW_SUB), :] = val[:, s * LANE:(s + 1) * LANE]


def _load_token_tiles(ref, n):
    return jnp.concatenate([ref[pl.ds(s, n, stride=ROW_SUB), :] for s in range(ROW_SUB)], axis=1)


def _tile_copy(src_ref, src_row, dst_ref, dst_row, sem):
    return pltpu.make_async_copy(src_ref.at[pl.ds(pl.multiple_of(src_row, ROW_SUB), ROW_SUB)],
                                 dst_ref.at[pl.ds(pl.multiple_of(dst_row, ROW_SUB), ROW_SUB)], sem)


def _dispatch_kernel(dest_ref, f_ref, xb_ref, sem):
    def issue(r, carry):
        for k in range(TOP_K):
            _tile_copy(f_ref, r * ROW_SUB, xb_ref, dest_ref[0, r * TOP_K + k], sem).start()
        return carry

    lax.fori_loop(0, TOK_TILE, issue, 0, unroll=8)
    for _ in range(TOP_K):
        pltpu.make_async_copy(f_ref, xb_ref.at[pl.ds(0, TOK_TILE * ROW_SUB)], sem).wait()


def _dispatch(dest3, f, n_rows_out):
    n_rows = dest3.shape[0] * TOK_TILE
    return pl.pallas_call(
        _dispatch_kernel,
        grid=(n_rows // TOK_TILE,),
        in_specs=[pl.BlockSpec((None, 1, TOK_TILE * TOP_K), lambda i: (i, 0, 0), memory_space=pltpu.SMEM),
                  pl.BlockSpec((TOK_TILE * ROW_SUB, LANE), lambda i: (i, 0))],
        out_specs=pl.BlockSpec(memory_space=pl.ANY),
        out_shape=jax.ShapeDtypeStruct((n_rows_out * ROW_SUB, LANE), F32),
        scratch_shapes=[pltpu.SemaphoreType.DMA(())],
        compiler_params=_params("arbitrary"),
        name="moe_dispatch",
    )(dest3, f)


def _moe_kernel(layer, be_ref, nv_ref, first_ref, nxt_ref, x_ref, wgu_hbm, bgu_ref, wd_hbm, bd_ref, o_ref,
                wgu_f32, wd_f32, wgu_s, wd_s, sems):
    i = pl.program_id(0)

    def fetch(e):
        return (pltpu.make_async_copy(wgu_hbm.at[layer, e], wgu_f32, sems.at[0]),
                pltpu.make_async_copy(wd_hbm.at[layer, e], wd_f32, sems.at[1]))

    @pl.when(i == 0)
    def _():
        for cp in fetch(be_ref[0]):
            cp.start()

    @pl.when(first_ref[i] == 1)
    def _():
        for cp in fetch(be_ref[i]):
            cp.wait()
        wgu_s[...] = wgu_f32[...].astype(BF16)
        wd_s[...] = wd_f32[...].astype(BF16)

        @pl.when(nxt_ref[i] >= 0)
        def _():
            for cp in fetch(nxt_ref[i]):
                cp.start()

    @pl.when(nv_ref[i] == 0)
    def _():
        o_ref[...] = jnp.zeros_like(o_ref)

    @pl.when(nv_ref[i] > 0)
    def _():
        row = lax.broadcasted_iota(jnp.int32, (MOE_ROWS, D_MODEL), 0)
        x = _load_token_tiles(x_ref, MOE_ROWS)
        x = jnp.where(row < nv_ref[i], x, 0.0).astype(BF16)
        gu = jnp.dot(x, wgu_s[...], preferred_element_type=F32) + bgu_ref[...]
        gate = jnp.minimum(gu[:, :D_EXPERT], SWIGLU_LIMIT)
        up = jnp.clip(gu[:, D_EXPERT:], -SWIGLU_LIMIT, SWIGLU_LIMIT)
        act = (up + 1.0) * (gate * jax.nn.sigmoid(SWIGLU_ALPHA * gate))
        _store_token_tiles(o_ref, jnp.dot(act.astype(BF16), wd_s[...], preferred_element_type=F32) + bd_ref[...])


def _moe_experts(layer, block_expert, rows_valid, first, nxt, xb, w_gu, b_gu, w_down, b_down):
    n_rows = xb.shape[0] // ROW_SUB
    bsel = lambda i, be, nv, fi, nx: (layer, be[i], 0, 0)
    rows = lambda i, be, nv, fi, nx: (i, 0)
    grid_spec = pltpu.PrefetchScalarGridSpec(
        num_scalar_prefetch=4,
        grid=(n_rows // MOE_ROWS,),
        in_specs=[
            pl.BlockSpec((MOE_ROWS * ROW_SUB, LANE), rows),
            pl.BlockSpec(memory_space=pl.ANY),
            pl.BlockSpec((None, None, 1, 2 * D_EXPERT), bsel),
            pl.BlockSpec(memory_space=pl.ANY),
            pl.BlockSpec((None, None, 1, D_MODEL), bsel),
        ],
        out_specs=pl.BlockSpec((MOE_ROWS * ROW_SUB, LANE), rows),
        scratch_shapes=[pltpu.VMEM((D_MODEL, 2 * D_EXPERT), F32),
                        pltpu.VMEM((D_EXPERT, D_MODEL), F32),
                        pltpu.VMEM((D_MODEL, 2 * D_EXPERT), BF16),
                        pltpu.VMEM((D_EXPERT, D_MODEL), BF16),
                        pltpu.SemaphoreType.DMA((2,))],
    )
    return pl.pallas_call(
        functools.partial(_moe_kernel, layer),
        grid_spec=grid_spec,
        out_shape=jax.ShapeDtypeStruct((n_rows * ROW_SUB, LANE), F32),
        compiler_params=_params("arbitrary"),
        name="moe_experts",
    )(block_expert, rows_valid, first, nxt, xb, w_gu, b_gu.reshape(DEPTH, N_EXPERTS, 1, -1), w_down,
      b_down.reshape(DEPTH, N_EXPERTS, 1, -1))


def _combine_kernel(dest_ref, destn_ref, x_ref, g2_ref, info_ref, yb_ref, o_ref, buf, gate_ref, sems):
    i = pl.program_id(0)
    slot = i % 2

    def issue(d_ref, sl):
        def body(r, carry):
            for k in range(TOP_K):
                _tile_copy(yb_ref, d_ref[0, r * TOP_K + k], buf.at[sl, k], r * ROW_SUB, sems.at[sl]).start()
            return carry

        lax.fori_loop(0, TOK_TILE, body, 0, unroll=8)

    @pl.when(i == 0)
    def _():
        issue(dest_ref, 0)

    @pl.when(i + 1 < pl.num_programs(0))
    def _():
        issue(destn_ref, 1 - slot)

    for k in range(TOP_K):
        pltpu.make_async_copy(yb_ref.at[pl.ds(0, TOK_TILE * ROW_SUB)], buf.at[slot, k], sems.at[slot]).wait()
    info = info_ref[...]
    for k in range(TOP_K):
        gate_ref[k] = jnp.broadcast_to(info[:, GATE_LANE + k:GATE_LANE + k + 1], (TOK_TILE, LANE))
    for s in range(ROW_SUB):
        y = None
        for k in range(TOP_K):
            term = gate_ref[k] * buf[slot, k, pl.ds(s, TOK_TILE, stride=ROW_SUB), :]
            y = term if y is None else y + term
        sl = slice(s * LANE, (s + 1) * LANE)
        o_ref[:, sl] = x_ref[:, sl] + g2_ref[:, sl] * y


def _combine(dest3, x, mods4, info, yb):
    tiles = dest3.shape[0]
    n_rows = tiles * TOK_TILE
    row = lambda n: pl.BlockSpec((TOK_TILE, n), lambda i: (i, 0))
    dest_spec = lambda f: pl.BlockSpec((None, 1, TOK_TILE * TOP_K), f, memory_space=pltpu.SMEM)
    return pl.pallas_call(
        _combine_kernel,
        grid=(tiles,),
        in_specs=[dest_spec(lambda i: (i, 0, 0)), dest_spec(lambda i: (jnp.minimum(i + 1, tiles - 1), 0, 0)),
                  row(D_MODEL), _mod_spec(5, TOK_TILE), row(LANE), pl.BlockSpec(memory_space=pl.ANY)],
        out_specs=row(D_MODEL),
        out_shape=jax.ShapeDtypeStruct((n_rows, D_MODEL), F32),
        scratch_shapes=[pltpu.VMEM((2, TOP_K, TOK_TILE * ROW_SUB, LANE), F32),
                        pltpu.VMEM((TOP_K, TOK_TILE, LANE), F32),
                        pltpu.SemaphoreType.DMA((2,))],
        compiler_params=_params("arbitrary"),
        name="moe_combine",
    )(dest3, dest3, x, mods4, info, yb)


def _moe_layer(layer, x_new, f, logits, n_rows, mods4, w_gu, b_gu, w_down, b_down):
    info, cnt = _route(logits, n_rows)
    idx = info[:, IDX_LANE:IDX_LANE + TOP_K].astype(jnp.int32)
    pos = info[:, POS_LANE:POS_LANE + TOP_K].astype(jnp.int32)
    counts = cnt[0, :N_EXPERTS].astype(jnp.int32)
    padded = (counts + MOE_ROWS - 1) // MOE_ROWS * MOE_ROWS
    pad_ends = jnp.cumsum(padded)
    pad_starts = pad_ends - padded
    experts = jnp.arange(N_EXPERTS, dtype=jnp.int32)
    dest = pos + jnp.sum(jnp.where(idx[:, :, None] == experts, pad_starts, 0), axis=-1)
    dest3 = (dest * ROW_SUB).reshape(n_rows // TOK_TILE, 1, TOK_TILE * TOP_K)
    n_blocks = n_rows * TOP_K // MOE_ROWS + N_EXPERTS
    block_start = jnp.arange(n_blocks, dtype=jnp.int32) * MOE_ROWS
    used = counts > 0
    last_used = jnp.max(jnp.where(used, experts, 0))
    block_expert = jnp.minimum(jnp.sum((block_start[:, None] >= pad_ends[None, :]).astype(jnp.int32), axis=1),
                               last_used)
    group_end = (pad_starts + counts)[block_expert]
    rows_valid = jnp.clip(group_end - block_start, 0, MOE_ROWS).astype(jnp.int32)
    first = jnp.concatenate([jnp.ones((1,), jnp.int32),
                             (block_expert[1:] != block_expert[:-1]).astype(jnp.int32)])
    later = jnp.where(used[None, :] & (experts[None, :] > experts[:, None]), experts[None, :], N_EXPERTS)
    next_used = jnp.min(later, axis=1)
    nxt = jnp.where(next_used < N_EXPERTS, next_used, -1)[block_expert].astype(jnp.int32)
    xb = _dispatch(dest3, f, n_blocks * MOE_ROWS)
    yb = _moe_experts(layer, block_expert, rows_valid, first, nxt, xb, w_gu, b_gu, w_down, b_down)
    return _combine(dest3, x_new, mods4, info, yb)


def _pad_lanes(v, start=0):
    return jnp.zeros((1, LANE), F32).at[0, start:start + v.shape[0]].set(v)


def kernel(x, c, ctx, c_ctx, mod_w, mod_b, norm_mix_g, norm_ffn_g, ab_w_in, ab_w_out, a_q_norm, a_k_norm, a_sink, b_dw_w, b_dw_b, b_ln_g, b_ln_b, c_w_in, c_q_a_norm, c_kv_a_norm, c_w_q_b, c_w_kv_b, c_q_norm, c_k_norm, c_kr_norm, c_w_out, router_w, router_b, exp_w_gu, exp_b_gu, exp_w_down, exp_b_down):
    cos_a, sin_a = _rope_lane_tables(HEAD_DIM, 0, LANE // HEAD_DIM)
    cos_c, sin_c = _rope_lane_tables(MLA_ROPE, MLA_NOPE, 1)
    cvec = jnp.concatenate([c, c_ctx[None, :], jnp.zeros((SEG_PAD - N_SEG, D_MODEL), F32)], axis=0)
    mods = _modulation(cvec, mod_w, mod_b)
    xt = jnp.concatenate([x.reshape(N_LAT, D_MODEL), ctx.reshape(N_CTX, D_MODEL)], axis=0)
    for layer in range(DEPTH):
        keep_ctx = layer < DEPTH - 1
        n_rows = N_TOK if keep_ctx else N_LAT
        mods4 = mods[layer].reshape(SEG_PAD, 6, 1, D_MODEL)
        i = layer // 2
        if layer % 2 == 0:
            qg2 = jnp.tile(a_q_norm[i], LANE // HEAD_DIM).reshape(1, LANE)
            kg2 = jnp.tile(a_k_norm[i], LANE // HEAD_DIM).reshape(1, LANE)
            q, k, v, z = _even_inproj(xt, norm_mix_g[layer], mods4, ab_w_in[i].astype(BF16), qg2, kg2, cos_a, sin_a)
            sink_b = jnp.broadcast_to(a_sink[i].astype(F32)[:, None], (A_Q_HEADS, LANE))
            a = _win_attn(q, k, v, sink_b)
            b = _conv_module(z, b_dw_w[i], b_dw_b[i], b_ln_g[i], b_ln_b[i])
            w_out = ab_w_out[i].astype(BF16)
            a_list, w_list = [a, b], [w_out[:DQ], w_out[DQ:]]
        else:
            w_in = c_w_in[i]
            zeros = lambda n: jnp.zeros((D_MODEL, n), F32)
            w_in_p = jnp.concatenate([w_in[:, :Q_LORA + KV_LORA], zeros(MLA_NOPE), w_in[:, Q_LORA + KV_LORA:],
                                      zeros(LANE - MLA_QK)], axis=1).astype(BF16)
            w_q = jnp.pad(c_w_q_b[i].reshape(Q_LORA, MLA_HEADS, MLA_QK),
                          ((0, 0), (0, 0), (0, MLA_PAD - MLA_QK))).reshape(Q_LORA, MLA_HEADS * MLA_PAD)
            w_kv = c_w_kv_b[i].reshape(KV_LORA, MLA_HEADS, MLA_NOPE + MLA_V)
            w_k = jnp.pad(w_kv[:, :, :MLA_NOPE], ((0, 0), (0, 0), (0, MLA_PAD - MLA_NOPE)))
            w_kv_p = jnp.concatenate([w_k.reshape(KV_LORA, -1), w_kv[:, :, MLA_NOPE:].reshape(KV_LORA, -1)], axis=1)
            q, k, v = _mla_proj(xt, norm_mix_g[layer], mods4, w_in_p, c_q_a_norm[i].reshape(1, -1),
                                c_kv_a_norm[i].reshape(1, -1), _pad_lanes(c_kr_norm[i], MLA_NOPE),
                                w_q.astype(BF16), w_kv_p.astype(BF16), _pad_lanes(c_q_norm[i]),
                                _pad_lanes(c_k_norm[i]), cos_c, sin_c)
            o = _mla_attn_latent(q, k, v, n_rows)
            if keep_ctx:
                o = _mla_attn_ctx(q, k, v, o)
            a_list, w_list = [o], [c_w_out[i].astype(BF16)]
        rw_pad = jnp.pad(router_w[layer], ((0, 0), (0, LANE - N_EXPERTS)))
        rb_pad = jnp.pad(router_b[layer], (0, LANE - N_EXPERTS)).reshape(1, LANE)
        x_new, f, logits = _outproj(a_list, w_list, xt, n_rows, mods4, norm_ffn_g[layer], rw_pad, rb_pad)
        xt = _moe_layer(layer, x_new, f, logits, n_rows, mods4, exp_w_gu, exp_b_gu, exp_w_down, exp_b_down)
    return xt.reshape(BATCH, SEQ, D_MODEL)
```

```python
import functools

import jax
import jax.numpy as jnp
import numpy as np
from jax import lax
from jax.experimental import pallas as pl
from jax.experimental.pallas import tpu as pltpu

D_MODEL = 1024
BATCH = 4
SEQ = 4096
DEPTH = 4
GRID_W = 64
CTX_LEN = 256
A_Q_HEADS = 8
A_KV_HEADS = 2
HEAD_DIM = 64
WINDOW = 128
CONV_CH = 512
CONV_WIDTH = 31
MLA_HEADS = 16
MLA_NOPE = 64
MLA_ROPE = 32
MLA_V = 64
MLA_QK = MLA_NOPE + MLA_ROPE
Q_LORA = 768
KV_LORA = 256
N_EXPERTS = 32
TOP_K = 4
D_EXPERT = 1024
SWIGLU_LIMIT = 7.0
SWIGLU_ALPHA = 1.702
ROPE_THETA = 10000.0
EPS = 1e-6
NEG_INF = -1e30

LANE = 128
N_LAT = BATCH * SEQ
N_CTX = BATCH * CTX_LEN
N_TOK = N_LAT + N_CTX
N_SEG = BATCH + 1
SEG_PAD = 8
DQ = A_Q_HEADS * HEAD_DIM
DKV = A_KV_HEADS * HEAD_DIM
DKV_DUP = A_KV_HEADS * LANE
AB_IN = DQ + 2 * DKV + 2 * CONV_CH
MLA_PAD = LANE
C_IN_PAD = Q_LORA + KV_LORA + LANE
MLA_KEYS = SEQ + CTX_LEN

TOK_TILE = 256
LAT_TILES = N_LAT // TOK_TILE
TILES_PER_SEQ = SEQ // TOK_TILE
OUT_TILE = 512
MOE_ROWS = 256
CONV_HALO = 16
VMEM_LIMIT = 56 * 1024 * 1024

F32 = jnp.float32
BF16 = jnp.bfloat16


def _params(*sem):
    return pltpu.CompilerParams(dimension_semantics=sem, vmem_limit_bytes=VMEM_LIMIT)


def _seg_of_tile(i, tile):
    return jnp.minimum(i * tile // SEQ, BATCH)


def _rope_block_of_tile(i):
    return jnp.where(i < LAT_TILES, i % TILES_PER_SEQ, TILES_PER_SEQ)


def _mod_spec(chunk, tile):
    return pl.BlockSpec((None, None, 1, D_MODEL), lambda i: (_seg_of_tile(i, tile), chunk, 0, 0))


def _full(shape):
    return pl.BlockSpec(shape, lambda *_: (0,) * len(shape))


def _rms(x, n=None):
    n = x.shape[-1] if n is None else n
    return x * lax.rsqrt(jnp.sum(x * x, axis=-1, keepdims=True) * (1.0 / n) + EPS)


def _modulated_norm(x_ref, g_ref, sh_ref, sc_ref):
    h = _rms(x_ref[...]) * g_ref[...]
    return h * (1.0 + sc_ref[...]) + sh_ref[...]


def _mod_kernel(c_ref, w_ref, b_ref, o_ref):
    c = c_ref[...]
    s = c * jax.nn.sigmoid(c)
    o_ref[...] = jnp.dot(s, w_ref[...], precision=lax.Precision.HIGHEST,
                         preferred_element_type=F32) + b_ref[...]


def _modulation(cvec, mod_w, mod_b):
    tn = 1536
    return pl.pallas_call(
        _mod_kernel,
        grid=(DEPTH, 6 * D_MODEL // tn),
        in_specs=[
            pl.BlockSpec((SEG_PAD, D_MODEL), lambda l, j: (0, 0)),
            pl.BlockSpec((None, D_MODEL, tn), lambda l, j: (l, 0, j)),
            pl.BlockSpec((None, 1, tn), lambda l, j: (l, 0, j)),
        ],
        out_specs=pl.BlockSpec((None, SEG_PAD, tn), lambda l, j: (l, 0, j)),
        out_shape=jax.ShapeDtypeStruct((DEPTH, SEG_PAD, 6 * D_MODEL), F32),
        compiler_params=_params("arbitrary", "arbitrary"),
        name="modulation",
    )(cvec, mod_w, mod_b.reshape(DEPTH, 1, 6 * D_MODEL))


def _rope_lanes(t, cos, sin, half0, shift):
    partner = jnp.where(half0, pltpu.roll(t, LANE - shift, 1), pltpu.roll(t, shift, 1))
    return t * cos + partner * sin


def _rope_lane_tables(rot_dims, lane_start, repeat):
    f32 = np.float32
    rows = SEQ // GRID_W
    row = np.repeat(np.arange(rows), GRID_W).astype(f32)
    col = np.tile(np.arange(GRID_W), rows).astype(f32)
    axis_dims = rot_dims // 2
    n = axis_dims // 2
    inv = (f32(ROPE_THETA) ** (-(np.arange(n, dtype=f32) * f32(2.0)) / f32(axis_dims))).astype(f32)
    ang = np.concatenate([row[:, None] * inv, col[:, None] * inv], axis=-1)
    d = np.arange(rot_dims)
    src = (d // (2 * n)) * n + d % n
    sign = np.where((d % (2 * n)) // n == 0, -1.0, 1.0).astype(f32)
    cos_t = np.ones((SEQ + TOK_TILE, LANE), f32)
    sin_t = np.zeros((SEQ + TOK_TILE, LANE), f32)
    for r in range(repeat):
        lo = lane_start + r * rot_dims
        cos_t[:SEQ, lo:lo + rot_dims] = np.cos(ang)[:, src]
        sin_t[:SEQ, lo:lo + rot_dims] = np.sin(ang)[:, src] * sign
    return jnp.asarray(cos_t), jnp.asarray(sin_t)


def _even_inproj_kernel(x_ref, g_ref, sh_ref, sc_ref, w_ref, qg_ref, kg_ref, cos_ref, sin_ref,
                        q_ref, k_ref, v_ref, z_ref):
    h = _modulated_norm(x_ref, g_ref, sh_ref, sc_ref)
    p = jnp.dot(h.astype(BF16), w_ref[...], preferred_element_type=F32)
    lane = lax.broadcasted_iota(jnp.int32, (TOK_TILE, LANE), 1)
    lo = lane < HEAD_DIM
    half0 = (lane & (HEAD_DIM // 2 - 1)) < HEAD_DIM // 4
    cos = cos_ref[...]
    sin = sin_ref[...]

    def norm_rope(t, gvec):
        ss = t * t
        s_lo = jnp.sum(jnp.where(lo, ss, 0.0), axis=-1, keepdims=True)
        s_hi = jnp.sum(jnp.where(lo, 0.0, ss), axis=-1, keepdims=True)
        ms = jnp.where(lo, s_lo, s_hi) * (1.0 / HEAD_DIM)
        tn = t * lax.rsqrt(ms + EPS) * gvec
        return _rope_lanes(tn, cos, sin, half0, HEAD_DIM // 4)

    scale = HEAD_DIM ** -0.5
    for j in range(DQ // LANE):
        q_ref[:, j * LANE:(j + 1) * LANE] = (norm_rope(p[:, j * LANE:(j + 1) * LANE], qg_ref[...]) * scale).astype(BF16)

    def twice(t, ref):
        sw = pltpu.roll(t, HEAD_DIM, 1)
        ref[:, :LANE] = jnp.where(lo, t, sw).astype(BF16)
        ref[:, LANE:] = jnp.where(lo, sw, t).astype(BF16)

    twice(norm_rope(p[:, DQ:DQ + DKV], kg_ref[...]), k_ref)
    twice(p[:, DQ + DKV:DQ + 2 * DKV], v_ref)
    u0 = DQ + 2 * DKV
    z_ref[...] = (p[:, u0:u0 + CONV_CH] * jax.nn.sigmoid(p[:, u0 + CONV_CH:])).astype(BF16)


def _even_inproj(xt, g, mods4, w_bf, qg2, kg2, cos_t, sin_t):
    row = lambda n: pl.BlockSpec((TOK_TILE, n), lambda i: (i, 0))
    tab = pl.BlockSpec((TOK_TILE, LANE), lambda i: (_rope_block_of_tile(i), 0))
    return pl.pallas_call(
        _even_inproj_kernel,
        grid=(N_TOK // TOK_TILE,),
        in_specs=[row(D_MODEL), _full((1, D_MODEL)), _mod_spec(0, TOK_TILE), _mod_spec(1, TOK_TILE),
                  _full((D_MODEL, AB_IN)), _full((1, LANE)), _full((1, LANE)), tab, tab],
        out_specs=[row(DQ), row(DKV_DUP), row(DKV_DUP), row(CONV_CH)],
        out_shape=[jax.ShapeDtypeStruct((N_TOK, DQ), BF16),
                   jax.ShapeDtypeStruct((N_TOK, DKV_DUP), BF16),
                   jax.ShapeDtypeStruct((N_TOK, DKV_DUP), BF16),
                   jax.ShapeDtypeStruct((N_TOK, CONV_CH), BF16)],
        compiler_params=_params("parallel"),
        name="even_inproj",
    )(xt, g.reshape(1, D_MODEL), mods4, mods4, w_bf, qg2, kg2, cos_t, sin_t)


def _softmax_pv(s, sk, v):
    m = jnp.maximum(jnp.max(s, axis=-1, keepdims=True), sk)
    p = jnp.exp(s - m)
    l = jnp.sum(p, axis=-1, keepdims=True) + jnp.exp(sk - m)
    o = jnp.dot(p.astype(BF16), v, preferred_element_type=F32)
    return o / l


def _nt_dot(a, b):
    return lax.dot_general(a, b, (((1,), (1,)), ((), ())), preferred_element_type=F32)


N_WIN = SEQ // WINDOW
CTX_QBLOCKS = CTX_LEN // WINDOW


def _win_attn_kernel(q_ref, kp_ref, kc_ref, kn_ref, vp_ref, vc_ref, vn_ref, kx_ref, vx_ref,
                     sink_ref, o_ref):
    n = pl.program_id(1)
    w = WINDOW
    k_all = jnp.concatenate([kp_ref[...], kc_ref[...], kn_ref[...], kx_ref[...]], axis=0)
    v_all = jnp.concatenate([vp_ref[...], vc_ref[...], vn_ref[...], vx_ref[...]], axis=0)
    nk = 3 * w + CTX_LEN
    group = A_Q_HEADS // A_KV_HEADS
    row = lax.broadcasted_iota(jnp.int32, (group * w, nk), 0)
    qi = row & (w - 1)
    ki = lax.broadcasted_iota(jnp.int32, (group * w, nk), 1)
    rel = ki - w - qi
    kpos = (n - 1) * w + ki
    kpos_end = jnp.where(n < N_WIN, SEQ, -1)
    valid = (ki >= 3 * w) | ((jnp.abs(rel) <= w) & (kpos >= 0) & (kpos < kpos_end))
    lo = lax.broadcasted_iota(jnp.int32, (w, LANE), 1) < HEAD_DIM
    head_of_row = lax.broadcasted_iota(jnp.int32, (group * w, 1), 0) // w
    tiles_per_kv = group * HEAD_DIM // LANE
    for g in range(A_KV_HEADS):
        kk = k_all[:, g * LANE:(g + 1) * LANE]
        vv = v_all[:, g * LANE:(g + 1) * LANE]
        parts = []
        for j in range(g * tiles_per_kv, (g + 1) * tiles_per_kv):
            qt = q_ref[:, j * LANE:(j + 1) * LANE]
            zero = jnp.zeros_like(qt)
            parts += [jnp.where(lo, qt, zero), jnp.where(lo, zero, qt)]
        qs = jnp.concatenate(parts, axis=0)
        sk = sink_ref[g * group:g * group + 1, 0:1]
        for hh in range(1, group):
            sk = jnp.where(head_of_row == hh, sink_ref[g * group + hh:g * group + hh + 1, 0:1], sk)
        s = jnp.where(valid, _nt_dot(qs, kk), NEG_INF)
        o = _softmax_pv(s, sk, vv)
        for t in range(tiles_per_kv):
            j = g * tiles_per_kv + t
            o_ref[:, j * LANE:(j + 1) * LANE] = jnp.where(
                lo, o[2 * t * w:(2 * t + 1) * w], o[(2 * t + 1) * w:(2 * t + 2) * w]).astype(o_ref.dtype)


def _win_attn(q, k, v, sink_b):
    def qblk(b, n):
        return (jnp.where(n < N_WIN, b * N_WIN + n, BATCH * N_WIN + b * CTX_QBLOCKS + (n - N_WIN)), 0)

    def local(off):
        return lambda b, n: (b * N_WIN + jnp.clip(n + off, 0, N_WIN - 1), 0)

    kv_spec = lambda f: pl.BlockSpec((WINDOW, DKV_DUP), f)
    ctx_spec = pl.BlockSpec((CTX_LEN, DKV_DUP), lambda b, n: (LAT_TILES + b, 0))
    return pl.pallas_call(
        _win_attn_kernel,
        grid=(BATCH, N_WIN + CTX_QBLOCKS),
        in_specs=[pl.BlockSpec((WINDOW, DQ), qblk),
                  kv_spec(local(-1)), kv_spec(local(0)), kv_spec(local(1)),
                  kv_spec(local(-1)), kv_spec(local(0)), kv_spec(local(1)),
                  ctx_spec, ctx_spec, _full((A_Q_HEADS, LANE))],
        out_specs=pl.BlockSpec((WINDOW, DQ), qblk),
        out_shape=jax.ShapeDtypeStruct((N_TOK, DQ), BF16),
        compiler_params=_params("parallel", "parallel"),
        name="win_attn",
    )(q, k, k, k, v, v, v, k, v, sink_b)


F32_SUBLANES = 8


def _conv_kernel(zp_ref, zc_ref, zn_ref, w_ref, b_ref, g_ref, bb_ref, o_ref, pad_ref, shift_ref):
    i = pl.program_id(0)
    tl = TOK_TILE
    h = CONV_HALO
    pos = i % TILES_PER_SEQ
    first = (i >= LAT_TILES) | (pos == 0)
    last = (i >= LAT_TILES) | (pos == TILES_PER_SEQ - 1)
    pad_ref[0:h, :] = jnp.where(first, 0.0, zp_ref[...].astype(F32))
    pad_ref[h:h + tl, :] = zc_ref[...].astype(F32)
    pad_ref[h + tl:2 * h + tl, :] = jnp.where(last, 0.0, zn_ref[...].astype(F32))
    span = tl + 2 * h - F32_SUBLANES
    for r in range(1, F32_SUBLANES):
        shift_ref[r - 1] = pad_ref[r:r + span, :]

    def tap(j):
        start = h - CONV_WIDTH // 2 + j
        r = start % F32_SUBLANES
        src = pad_ref if r == 0 else shift_ref.at[r - 1]
        return src[start - r:start - r + tl, :] * w_ref[j:j + 1, :]

    acc = tap(0)
    for j in range(1, CONV_WIDTH):
        acc = acc + tap(j)
    z = acc + b_ref[...]
    mu = jnp.mean(z, axis=-1, keepdims=True)
    zc = z - mu
    y = zc * lax.rsqrt(jnp.mean(zc * zc, axis=-1, keepdims=True) + EPS)
    y = y * g_ref[...] + bb_ref[...]
    o_ref[...] = (y * jax.nn.sigmoid(y)).astype(o_ref.dtype)


def _conv_module(z, dw_w, dw_b, ln_g, ln_b):
    r = TOK_TILE // CONV_HALO
    n_halo = N_TOK // CONV_HALO
    halo = lambda f: pl.BlockSpec((CONV_HALO, CONV_CH), f)
    vec = _full((1, CONV_CH))
    return pl.pallas_call(
        _conv_kernel,
        grid=(N_TOK // TOK_TILE,),
        in_specs=[halo(lambda i: (jnp.maximum(i * r - 1, 0), 0)),
                  pl.BlockSpec((TOK_TILE, CONV_CH), lambda i: (i, 0)),
                  halo(lambda i: (jnp.minimum((i + 1) * r, n_halo - 1), 0)),
                  _full((CONV_WIDTH, CONV_CH)), vec, vec, vec],
        out_specs=pl.BlockSpec((TOK_TILE, CONV_CH), lambda i: (i, 0)),
        out_shape=jax.ShapeDtypeStruct((N_TOK, CONV_CH), BF16),
        scratch_shapes=[pltpu.VMEM((TOK_TILE + 2 * CONV_HALO, CONV_CH), F32),
                        pltpu.VMEM((F32_SUBLANES - 1, TOK_TILE + 2 * CONV_HALO - F32_SUBLANES, CONV_CH), F32)],
        compiler_params=_params("parallel"),
        name="conv_module",
    )(z, z, z, dw_w, dw_b.reshape(1, -1), ln_g.reshape(1, -1), ln_b.reshape(1, -1))


def _mla_proj_kernel(x_ref, g_ref, sh_ref, sc_ref, win_ref, qag_ref, kvag_ref, krg_ref, wq_ref, wkv_ref,
                     qg_ref, kg_ref, cos_ref, sin_ref, q_ref, k_ref, v_ref):
    h = _modulated_norm(x_ref, g_ref, sh_ref, sc_ref)
    p = jnp.dot(h.astype(BF16), win_ref[...], preferred_element_type=F32)
    lane = lax.broadcasted_iota(jnp.int32, (TOK_TILE, LANE), 1)
    half0 = (lane & (MLA_ROPE // 2 - 1)) < MLA_ROPE // 4
    cos = cos_ref[...]
    sin = sin_ref[...]
    rope = lambda t: _rope_lanes(t, cos, sin, half0, MLA_ROPE // 4)

    cq = _rms(p[:, :Q_LORA]) * qag_ref[...]
    ckv = _rms(p[:, Q_LORA:Q_LORA + KV_LORA]) * kvag_ref[...]
    kr = rope(_rms(p[:, Q_LORA + KV_LORA:], MLA_ROPE) * krg_ref[...])

    q = jnp.dot(cq.astype(BF16), wq_ref[...], preferred_element_type=F32)
    scale = float(MLA_QK ** -0.5 * np.log2(np.e))
    for hd in range(MLA_HEADS):
        qh = _rms(q[:, hd * MLA_PAD:(hd + 1) * MLA_PAD], MLA_QK) * qg_ref[...]
        q_ref[:, hd * MLA_PAD:(hd + 1) * MLA_PAD] = (rope(qh) * scale).astype(BF16)

    kv = jnp.dot(ckv.astype(BF16), wkv_ref[...], preferred_element_type=F32)
    for hd in range(MLA_HEADS):
        kh = _rms(kv[:, hd * MLA_PAD:(hd + 1) * MLA_PAD], MLA_NOPE) * kg_ref[...]
        k_ref[:, hd * MLA_PAD:(hd + 1) * MLA_PAD] = (kh + kr).astype(BF16)
    v_ref[...] = kv[:, MLA_HEADS * MLA_PAD:].astype(BF16)


def _mla_proj(xt, g, mods4, w_in, qa_g, kva_g, kr_g, w_q, w_kv, q_g, k_g, cos_t, sin_t):
    tab = pl.BlockSpec((TOK_TILE, LANE), lambda i: (_rope_block_of_tile(i), 0))

    def key_blk(i):
        return (jnp.where(i < LAT_TILES, i // TILES_PER_SEQ, i - LAT_TILES), _rope_block_of_tile(i), 0)

    kv_n = MLA_HEADS * (MLA_PAD + MLA_V)
    return pl.pallas_call(
        _mla_proj_kernel,
        grid=(N_TOK // TOK_TILE,),
        in_specs=[pl.BlockSpec((TOK_TILE, D_MODEL), lambda i: (i, 0)), _full((1, D_MODEL)),
                  _mod_spec(0, TOK_TILE), _mod_spec(1, TOK_TILE),
                  _full((D_MODEL, C_IN_PAD)), _full((1, Q_LORA)), _full((1, KV_LORA)), _full((1, LANE)),
                  _full((Q_LORA, MLA_HEADS * MLA_PAD)), _full((KV_LORA, kv_n)),
                  _full((1, LANE)), _full((1, LANE)), tab, tab],
        out_specs=[pl.BlockSpec((TOK_TILE, MLA_HEADS * MLA_PAD), lambda i: (i, 0)),
                   pl.BlockSpec((None, TOK_TILE, MLA_HEADS * MLA_PAD), key_blk),
                   pl.BlockSpec((None, TOK_TILE, MLA_HEADS * MLA_V), key_blk)],
        out_shape=[jax.ShapeDtypeStruct((N_TOK, MLA_HEADS * MLA_PAD), BF16),
                   jax.ShapeDtypeStruct((BATCH, MLA_KEYS, MLA_HEADS * MLA_PAD), BF16),
                   jax.ShapeDtypeStruct((BATCH, MLA_KEYS, MLA_HEADS * MLA_V), BF16)],
        compiler_params=_params("parallel"),
        name="mla_proj",
    )(xt, g.reshape(1, D_MODEL), mods4, mods4, w_in, qa_g, kva_g, kr_g, w_q, w_kv, q_g, k_g, cos_t, sin_t)


MLA_GROUP = 8
MLA_Q_TILE = 256


def _mla_kernel(q_ref, k_ref, v_ref, *rest):
    o_ref = rest[-1]
    lane = lax.broadcasted_iota(jnp.int32, (q_ref.shape[0], 2 * MLA_V), 1)
    for jp in range(MLA_GROUP // 2):
        outs = []
        v2 = v_ref[:, jp * 2 * MLA_V:(jp + 1) * 2 * MLA_V]
        for j in range(2 * jp, 2 * jp + 2):
            qh = q_ref[:, j * MLA_PAD:(j + 1) * MLA_PAD]
            kh = k_ref[:, j * MLA_PAD:(j + 1) * MLA_PAD]
            s = _nt_dot(qh, kh)
            m = jnp.max(s, axis=-1, keepdims=True)
            p = jnp.exp2(s - m)
            l = jnp.sum(p, axis=-1, keepdims=True)
            outs.append(jnp.dot(p.astype(BF16), v2, preferred_element_type=F32) / l)
        o_ref[:, jp * 2 * MLA_V:(jp + 1) * 2 * MLA_V] = jnp.where(lane < MLA_V, outs[0], outs[1]).astype(o_ref.dtype)


def _mla_attn_latent(q, k, v, n_out_rows):
    nq = SEQ // MLA_Q_TILE
    qblk = lambda b, p, i: (b * nq + i, p)
    return pl.pallas_call(
        _mla_kernel,
        grid=(BATCH, MLA_HEADS // MLA_GROUP, nq),
        in_specs=[pl.BlockSpec((MLA_Q_TILE, MLA_GROUP * MLA_PAD), qblk),
                  pl.BlockSpec((None, MLA_KEYS, MLA_GROUP * MLA_PAD), lambda b, p, i: (b, 0, p)),
                  pl.BlockSpec((None, MLA_KEYS, MLA_GROUP * MLA_V), lambda b, p, i: (b, 0, p))],
        out_specs=pl.BlockSpec((MLA_Q_TILE, MLA_GROUP * MLA_V), qblk),
        out_shape=jax.ShapeDtypeStruct((n_out_rows, MLA_HEADS * MLA_V), BF16),
        compiler_params=_params("parallel", "parallel", "parallel"),
        name="mla_attn",
    )(q, k, v)


def _mla_attn_ctx(q, k, v, o):
    qblk = lambda b, p: (LAT_TILES + b, p)
    tail = lambda b, p: (b, TILES_PER_SEQ, p)
    return pl.pallas_call(
        _mla_kernel,
        grid=(BATCH, MLA_HEADS // MLA_GROUP),
        in_specs=[pl.BlockSpec((TOK_TILE, MLA_GROUP * MLA_PAD), qblk),
                  pl.BlockSpec((None, CTX_LEN, MLA_GROUP * MLA_PAD), tail),
                  pl.BlockSpec((None, CTX_LEN, MLA_GROUP * MLA_V), tail),
                  pl.BlockSpec(memory_space=pl.ANY)],
        out_specs=pl.BlockSpec((TOK_TILE, MLA_GROUP * MLA_V), qblk),
        out_shape=jax.ShapeDtypeStruct((N_TOK, MLA_HEADS * MLA_V), BF16),
        input_output_aliases={3: 0},
        compiler_params=_params("parallel", "parallel"),
        name="mla_attn_ctx",
    )(q, k, v, o)


def _split_bf16(a):
    hi = a.astype(BF16)
    return hi, (a - hi.astype(F32)).astype(BF16)


def _outproj_kernel(n_in, *refs):
    a_refs = refs[:n_in]
    w_refs = refs[n_in:2 * n_in]
    x_ref, g1_ref, gn_ref, sh_ref, sc_ref, rw_ref, rb_ref, xo_ref, f_ref, lg_ref = refs[2 * n_in:]
    y = jnp.dot(a_refs[0][...], w_refs[0][...], preferred_element_type=F32)
    for a_ref, w_ref in zip(a_refs[1:], w_refs[1:]):
        y = y + jnp.dot(a_ref[...], w_ref[...], preferred_element_type=F32)
    xo_ref[...] = x_ref[...] + g1_ref[...] * y
    f = _modulated_norm(xo_ref, gn_ref, sh_ref, sc_ref)
    _store_token_tiles(f_ref, f)
    f_hi, f_lo = _split_bf16(f)
    w_hi, w_lo = _split_bf16(rw_ref[...])
    mm = lambda a, b: jnp.dot(a, b, preferred_element_type=F32)
    lg_ref[...] = mm(f_hi, w_hi) + (mm(f_lo, w_hi) + mm(f_hi, w_lo)) + rb_ref[...]


def _outproj(a_list, w_list, x, n_rows, mods4, gn, rw_pad, rb_pad):
    n_in = len(a_list)
    row = lambda n: pl.BlockSpec((OUT_TILE, n), lambda i: (i, 0))
    in_specs = [row(a.shape[1]) for a in a_list] + [_full(w.shape) for w in w_list]
    in_specs += [row(D_MODEL), _mod_spec(2, OUT_TILE), _full((1, D_MODEL)),
                 _mod_spec(3, OUT_TILE), _mod_spec(4, OUT_TILE), _full((D_MODEL, LANE)), _full((1, LANE))]
    return pl.pallas_call(
        functools.partial(_outproj_kernel, n_in),
        grid=(n_rows // OUT_TILE,),
        in_specs=in_specs,
        out_specs=[row(D_MODEL), pl.BlockSpec((OUT_TILE * ROW_SUB, LANE), lambda i: (i, 0)), row(LANE)],
        out_shape=[jax.ShapeDtypeStruct((n_rows, D_MODEL), F32),
                   jax.ShapeDtypeStruct((n_rows * ROW_SUB, LANE), F32),
                   jax.ShapeDtypeStruct((n_rows, LANE), F32)],
        compiler_params=_params("parallel"),
        name="outproj",
    )(*a_list, *w_list, x, mods4, gn.reshape(1, D_MODEL), mods4, mods4, rw_pad, rb_pad)


IDX_LANE = 0
GATE_LANE = TOP_K
POS_LANE = 2 * TOP_K
MASKED = -3.0e38
ROUTE_TILE = 1024


def _route_kernel(lg_ref, info_ref, cnt_ref, carry_ref):
    i = pl.program_id(0)

    @pl.when(i == 0)
    def _():
        carry_ref[...] = jnp.zeros_like(carry_ref)

    lane = lax.broadcasted_iota(jnp.int32, (ROUTE_TILE, LANE), 1).astype(F32)
    x = jnp.where(lane < N_EXPERTS, lg_ref[...], MASKED)
    vals, idxs = [], []
    for _ in range(TOP_K):
        m = jnp.max(x, axis=-1, keepdims=True)
        idx = jnp.min(jnp.where(x == m, lane, float(LANE)), axis=-1, keepdims=True)
        vals.append(m)
        idxs.append(idx)
        x = jnp.where(lane == idx, MASKED, x)
    es = [jnp.exp(v - vals[0]) for v in vals]
    denom = es[0] + es[1] + es[2] + es[3]
    onehot = jnp.zeros((ROUTE_TILE, LANE), F32)
    for idx in idxs:
        onehot = onehot + jnp.where(lane == idx, 1.0, 0.0)
    r = lax.broadcasted_iota(jnp.int32, (ROUTE_TILE, ROUTE_TILE), 0)
    c = lax.broadcasted_iota(jnp.int32, (ROUTE_TILE, ROUTE_TILE), 1)
    tri = jnp.where(r > c, 1.0, 0.0).astype(BF16)
    before = jnp.dot(tri, onehot.astype(BF16), preferred_element_type=F32) + carry_ref[...]
    info = jnp.zeros((ROUTE_TILE, LANE), F32)
    for k in range(TOP_K):
        pos =jnp.sum(jnp.where(lane == idxs[k], before, 0.0), axis=-1, keepdims=True)
        info = jnp.where(lane == IDX_LANE + k, idxs[k], info)
        info = jnp.where(lane == GATE_LANE + k, es[k] / denom, info)
        info = jnp.where(lane == POS_LANE + k, pos, info)
    info_ref[...] = info
    carry_ref[...] = carry_ref[...] + jnp.sum(onehot, axis=0, keepdims=True)
    cnt_ref[...] = carry_ref[...]


def _route(logits, n_rows):
    return pl.pallas_call(
        _route_kernel,
        grid=(n_rows // ROUTE_TILE,),
        in_specs=[pl.BlockSpec((ROUTE_TILE, LANE), lambda i: (i, 0))],
        out_specs=[pl.BlockSpec((ROUTE_TILE, LANE), lambda i: (i, 0)), _full((1, LANE))],
        out_shape=[jax.ShapeDtypeStruct((n_rows, LANE), F32), jax.ShapeDtypeStruct((1, LANE), F32)],
        scratch_shapes=[pltpu.VMEM((1, LANE), F32)],
        compiler_params=_params("arbitrary"),
        name="route",
    )(logits)


ROW_SUB = D_MODEL // LANE


def _store_token_tiles(ref, val):
    n = val.shape[0]
    for s in range(ROW_SUB):
        ref[pl.ds(s, n, stride=ROW_SUB), :] = val[:, s * LANE:(s + 1) * LANE]


def _load_token_tiles(ref, n):
    return jnp.concatenate([ref[pl.ds(s, n, stride=ROW_SUB), :] for s in range(ROW_SUB)], axis=1)


def _tile_copy(src_ref, src_row, dst_ref, dst_row, sem):
    return pltpu.make_async_copy(src_ref.at[pl.ds(pl.multiple_of(src_row, ROW_SUB), ROW_SUB)],
                                 dst_ref.at[pl.ds(pl.multiple_of(dst_row, ROW_SUB), ROW_SUB)], sem)


def _dispatch_kernel(dest_ref, f_ref, xb_ref, sem):
    def issue(r, carry):
        for k in range(TOP_K):
            _tile_copy(f_ref, r * ROW_SUB, xb_ref, dest_ref[0, r * TOP_K + k], sem).start()
        return carry

    lax.fori_loop(0, TOK_TILE, issue, 0, unroll=8)
    for _ in range(TOP_K):
        pltpu.make_async_copy(f_ref, xb_ref.at[pl.ds(0, TOK_TILE * ROW_SUB)], sem).wait()


def _dispatch(dest3, f, n_rows_out):
    n_rows = dest3.shape[0] * TOK_TILE
    return pl.pallas_call(
        _dispatch_kernel,
        grid=(n_rows // TOK_TILE,),
        in_specs=[pl.BlockSpec((None, 1, TOK_TILE * TOP_K), lambda i: (i, 0, 0), memory_space=pltpu.SMEM),
                  pl.BlockSpec((TOK_TILE * ROW_SUB, LANE), lambda i: (i, 0))],
        out_specs=pl.BlockSpec(memory_space=pl.ANY),
        out_shape=jax.ShapeDtypeStruct((n_rows_out * ROW_SUB, LANE), F32),
        scratch_shapes=[pltpu.SemaphoreType.DMA(())],
        compiler_params=_params("arbitrary"),
        name="moe_dispatch",
    )(dest3, f)


def _moe_kernel(layer, be_ref, nv_ref, first_ref, nxt_ref, x_ref, wgu_hbm, bgu_ref, wd_hbm, bd_ref, o_ref,
                wgu_f32, wd_f32, wgu_s, wd_s, sems):
    i = pl.program_id(0)

    def fetch(e):
        return (pltpu.make_async_copy(wgu_hbm.at[layer, e], wgu_f32, sems.at[0]),
                pltpu.make_async_copy(wd_hbm.at[layer, e], wd_f32, sems.at[1]))

    @pl.when(i == 0)
    def _():
        for cp in fetch(be_ref[0]):
            cp.start()

    @pl.when(first_ref[i] == 1)
    def _():
        for cp in fetch(be_ref[i]):
            cp.wait()
        wgu_s[...] = wgu_f32[...].astype(BF16)
        wd_s[...] = wd_f32[...].astype(BF16)

        @pl.when(nxt_ref[i] >= 0)
        def _():
            for cp in fetch(nxt_ref[i]):
                cp.start()

    @pl.when(nv_ref[i] == 0)
    def _():
        o_ref[...] = jnp.zeros_like(o_ref)

    @pl.when(nv_ref[i] > 0)
    def _():
        row = lax.broadcasted_iota(jnp.int32, (MOE_ROWS, D_MODEL), 0)
        x = _load_token_tiles(x_ref, MOE_ROWS)
        x = jnp.where(row < nv_ref[i], x, 0.0).astype(BF16)
        gu = jnp.dot(x, wgu_s[...], preferred_element_type=F32) + bgu_ref[...]
        gate = jnp.minimum(gu[:, :D_EXPERT], SWIGLU_LIMIT)
        up = jnp.clip(gu[:, D_EXPERT:], -SWIGLU_LIMIT, SWIGLU_LIMIT)
        act = (up + 1.0) * (gate * jax.nn.sigmoid(SWIGLU_ALPHA * gate))
        _store_token_tiles(o_ref, jnp.dot(act.astype(BF16), wd_s[...], preferred_element_type=F32) + bd_ref[...])


def _moe_experts(layer, block_expert, rows_valid, first, nxt, xb, w_gu, b_gu, w_down, b_down):
    n_rows = xb.shape[0] // ROW_SUB
    bsel = lambda i, be, nv, fi, nx: (layer, be[i], 0, 0)
    rows = lambda i, be, nv, fi, nx: (i, 0)
    grid_spec = pltpu.PrefetchScalarGridSpec(
        num_scalar_prefetch=4,
        grid=(n_rows // MOE_ROWS,),
        in_specs=[
            pl.BlockSpec((MOE_ROWS * ROW_SUB, LANE), rows),
            pl.BlockSpec(memory_space=pl.ANY),
            pl.BlockSpec((None, None, 1, 2 * D_EXPERT), bsel),
            pl.BlockSpec(memory_space=pl.ANY),
            pl.BlockSpec((None, None, 1, D_MODEL), bsel),
        ],
        out_specs=pl.BlockSpec((MOE_ROWS * ROW_SUB, LANE), rows),
        scratch_shapes=[pltpu.VMEM((D_MODEL, 2 * D_EXPERT), F32),
                        pltpu.VMEM((D_EXPERT, D_MODEL), F32),
                        pltpu.VMEM((D_MODEL, 2 * D_EXPERT), BF16),
                        pltpu.VMEM((D_EXPERT, D_MODEL), BF16),
                        pltpu.SemaphoreType.DMA((2,))],
    )
    return pl.pallas_call(
        functools.partial(_moe_kernel, layer),
        grid_spec=grid_spec,
        out_shape=jax.ShapeDtypeStruct((n_rows * ROW_SUB, LANE), F32),
        compiler_params=_params("arbitrary"),
        name="moe_experts",
    )(block_expert, rows_valid, first, nxt, xb, w_gu, b_gu.reshape(DEPTH, N_EXPERTS, 1, -1), w_down,
      b_down.reshape(DEPTH, N_EXPERTS, 1, -1))


def _combine_kernel(dest_ref, destn_ref, x_ref, g2_ref, info_ref, yb_ref, o_ref, buf, gate_ref, sems):
    i = pl.program_id(0)
    slot = i % 2

    def issue(d_ref, sl):
        def body(r, carry):
            for k in range(TOP_K):
                _tile_copy(yb_ref, d_ref[0, r * TOP_K + k], buf.at[sl, k], r * ROW_SUB, sems.at[sl]).start()
            return carry

        lax.fori_loop(0, TOK_TILE, body, 0, unroll=8)

    @pl.when(i == 0)
    def _():
        issue(dest_ref, 0)

    @pl.when(i + 1 < pl.num_programs(0))
    def _():
        issue(destn_ref, 1 - slot)

    for k in range(TOP_K):
        pltpu.make_async_copy(yb_ref.at[pl.ds(0, TOK_TILE * ROW_SUB)], buf.at[slot, k], sems.at[slot]).wait()
    info = info_ref[...]
    for k in range(TOP_K):
        gate_ref[k] = jnp.broadcast_to(info[:, GATE_LANE + k:GATE_LANE + k + 1], (TOK_TILE, LANE))
    for s in range(ROW_SUB):
        y = None
        for k in range(TOP_K):
            term = gate_ref[k] * buf[slot, k, pl.ds(s, TOK_TILE, stride=ROW_SUB), :]
            y = term if y is None else y + term
        sl = slice(s * LANE, (s + 1) * LANE)
        o_ref[:, sl] = x_ref[:, sl] + g2_ref[:, sl] * y


def _combine(dest3, x, mods4, info, yb):
    tiles = dest3.shape[0]
    n_rows = tiles * TOK_TILE
    row = lambda n: pl.BlockSpec((TOK_TILE, n), lambda i: (i, 0))
    dest_spec = lambda f: pl.BlockSpec((None, 1, TOK_TILE * TOP_K), f, memory_space=pltpu.SMEM)
    return pl.pallas_call(
        _combine_kernel,
        grid=(tiles,),
        in_specs=[dest_spec(lambda i: (i, 0, 0)), dest_spec(lambda i: (jnp.minimum(i + 1, tiles - 1), 0, 0)),
                  row(D_MODEL), _mod_spec(5, TOK_TILE), row(LANE), pl.BlockSpec(memory_space=pl.ANY)],
        out_specs=row(D_MODEL),
        out_shape=jax.ShapeDtypeStruct((n_rows, D_MODEL), F32),
        scratch_shapes=[pltpu.VMEM((2, TOP_K, TOK_TILE * ROW_SUB, LANE), F32),
                        pltpu.VMEM((TOP_K, TOK_TILE, LANE), F32),
                        pltpu.SemaphoreType.DMA((2,))],
        compiler_params=_params("arbitrary"),
        name="moe_combine",
    )(dest3, dest3, x, mods4, info, yb)


def _moe_layer(layer, x_new, f, logits, n_rows, mods4, w_gu, b_gu, w_down, b_down):
    info, cnt = _route(logits, n_rows)
    idx = info[:, IDX_LANE:IDX_LANE + TOP_K].astype(jnp.int32)
    pos = info[:, POS_LANE:POS_LANE + TOP_K].astype(jnp.int32)
    counts = cnt[0, :N_EXPERTS].astype(jnp.int32)
    padded = (counts + MOE_ROWS - 1) // MOE_ROWS * MOE_ROWS
    pad_ends = jnp.cumsum(padded)
    pad_starts = pad_ends - padded
    experts = jnp.arange(N_EXPERTS, dtype=jnp.int32)
    dest = pos + jnp.sum(jnp.where(idx[:, :, None] == experts, pad_starts, 0), axis=-1)
    dest3 = (dest * ROW_SUB).reshape(n_rows // TOK_TILE, 1, TOK_TILE * TOP_K)
    n_blocks = n_rows * TOP_K // MOE_ROWS + N_EXPERTS
    block_start = jnp.arange(n_blocks, dtype=jnp.int32) * MOE_ROWS
    used = counts > 0
    last_used = jnp.max(jnp.where(used, experts, 0))
    block_expert = jnp.minimum(jnp.sum((block_start[:, None] >= pad_ends[None, :]).astype(jnp.int32), axis=1),
                               last_used)
    group_end = (pad_starts + counts)[block_expert]
    rows_valid = jnp.clip(group_end - block_start, 0, MOE_ROWS).astype(jnp.int32)
    first = jnp.concatenate([jnp.ones((1,), jnp.int32),
                             (block_expert[1:] != block_expert[:-1]).astype(jnp.int32)])
    later = jnp.where(used[None, :] & (experts[None, :] > experts[:, None]), experts[None, :], N_EXPERTS)
    next_used = jnp.min(later, axis=1)
    nxt = jnp.where(next_used < N_EXPERTS, next_used, -1)[block_expert].astype(jnp.int32)
    xb = _dispatch(dest3, f, n_blocks * MOE_ROWS)
    yb = _moe_experts(layer, block_expert, rows_valid, first, nxt, xb, w_gu, b_gu, w_down, b_down)
    return _combine(dest3, x_new, mods4, info, yb)


def _pad_lanes(v, start=0):
    return jnp.zeros((1, LANE), F32).at[0, start:start + v.shape[0]].set(v)


def kernel(x, c, ctx, c_ctx, mod_w, mod_b, norm_mix_g, norm_ffn_g, ab_w_in, ab_w_out, a_q_norm, a_k_norm, a_sink, b_dw_w, b_dw_b, b_ln_g, b_ln_b, c_w_in, c_q_a_norm, c_kv_a_norm, c_w_q_b, c_w_kv_b, c_q_norm, c_k_norm, c_kr_norm, c_w_out, router_w, router_b, exp_w_gu, exp_b_gu, exp_w_down, exp_b_down):
    cos_a, sin_a = _rope_lane_tables(HEAD_DIM, 0, LANE // HEAD_DIM)
    cos_c, sin_c = _rope_lane_tables(MLA_ROPE, MLA_NOPE, 1)
    cvec = jnp.concatenate([c, c_ctx[None, :], jnp.zeros((SEG_PAD - N_SEG, D_MODEL), F32)], axis=0)
    mods = _modulation(cvec, mod_w, mod_b)
    xt = jnp.concatenate([x.reshape(N_LAT, D_MODEL), ctx.reshape(N_CTX, D_MODEL)], axis=0)
    for layer in range(DEPTH):
        keep_ctx = layer < DEPTH - 1
        n_rows = N_TOK if keep_ctx else N_LAT
        mods4 = mods[layer].reshape(SEG_PAD, 6, 1, D_MODEL)
        i = layer // 2
        if layer % 2 == 0:
            qg2 = jnp.tile(a_q_norm[i], LANE // HEAD_DIM).reshape(1, LANE)
            kg2 = jnp.tile(a_k_norm[i], LANE // HEAD_DIM).reshape(1, LANE)
            q, k, v, z = _even_inproj(xt, norm_mix_g[layer], mods4, ab_w_in[i].astype(BF16), qg2, kg2, cos_a, sin_a)
            sink_b = jnp.broadcast_to(a_sink[i].astype(F32)[:, None], (A_Q_HEADS, LANE))
            a = _win_attn(q, k, v, sink_b)
            b = _conv_module(z, b_dw_w[i], b_dw_b[i], b_ln_g[i], b_ln_b[i])
            w_out = ab_w_out[i].astype(BF16)
            a_list, w_list = [a, b], [w_out[:DQ], w_out[DQ:]]
        else:
            w_in = c_w_in[i]
            zeros = lambda n: jnp.zeros((D_MODEL, n), F32)
            w_in_p = jnp.concatenate([w_in[:, :Q_LORA + KV_LORA], zeros(MLA_NOPE), w_in[:, Q_LORA + KV_LORA:],
                                      zeros(LANE - MLA_QK)], axis=1).astype(BF16)
            w_q = jnp.pad(c_w_q_b[i].reshape(Q_LORA, MLA_HEADS, MLA_QK),
                          ((0, 0), (0, 0), (0, MLA_PAD - MLA_QK))).reshape(Q_LORA, MLA_HEADS * MLA_PAD)
            w_kv = c_w_kv_b[i].reshape(KV_LORA, MLA_HEADS, MLA_NOPE + MLA_V)
            w_k = jnp.pad(w_kv[:, :, :MLA_NOPE], ((0, 0), (0, 0), (0, MLA_PAD - MLA_NOPE)))
            w_kv_p = jnp.concatenate([w_k.reshape(KV_LORA, -1), w_kv[:, :, MLA_NOPE:].reshape(KV_LORA, -1)], axis=1)
            q, k, v = _mla_proj(xt, norm_mix_g[layer], mods4, w_in_p, c_q_a_norm[i].reshape(1, -1),
                                c_kv_a_norm[i].reshape(1, -1), _pad_lanes(c_kr_norm[i], MLA_NOPE),
                                w_q.astype(BF16), w_kv_p.astype(BF16), _pad_lanes(c_q_norm[i]),
                                _pad_lanes(c_k_norm[i]), cos_c, sin_c)
            o = _mla_attn_latent(q, k, v, n_rows)
            if keep_ctx:
                o = _mla_attn_ctx(q, k, v, o)
            a_list, w_list = [o], [c_w_out[i].astype(BF16)]
        rw_pad = jnp.pad(router_w[layer], ((0, 0), (0, LANE - N_EXPERTS)))
        rb_pad = jnp.pad(router_b[layer], (0, LANE - N_EXPERTS)).reshape(1, LANE)
        x_new, f, logits = _outproj(a_list, w_list, xt, n_rows, mods4, norm_ffn_g[layer], rw_pad, rb_pad)
        xt = _moe_layer(layer, x_new, f, logits, n_rows, mods4, exp_w_gu, exp_b_gu, exp_w_down, exp_b_down)
    return xt.reshape(BATCH, SEQ, D_MODEL)
```

```python
import functools

import jax
import jax.numpy as jnp
import numpy as np
from jax import lax
from jax.experimental import pallas as pl
from jax.experimental.pallas import tpu as pltpu

D_MODEL = 1024
BATCH = 4
SEQ = 4096
DEPTH = 4
GRID_W = 64
CTX_LEN = 256
A_Q_HEADS = 8
A_KV_HEADS = 2
HEAD_DIM = 64
WINDOW = 128
CONV_CH = 512
CONV_WIDTH = 31
MLA_HEADS = 16
MLA_NOPE = 64
MLA_ROPE = 32
MLA_V = 64
MLA_QK = MLA_NOPE + MLA_ROPE
Q_LORA = 768
KV_LORA = 256
N_EXPERTS = 32
TOP_K = 4
D_EXPERT = 1024
SWIGLU_LIMIT = 7.0
SWIGLU_ALPHA = 1.702
ROPE_THETA = 10000.0
EPS = 1e-6
NEG_INF = -1e30

LANE = 128
N_LAT = BATCH * SEQ
N_CTX = BATCH * CTX_LEN
N_TOK = N_LAT + N_CTX
N_SEG = BATCH + 1
SEG_PAD = 8
DQ = A_Q_HEADS * HEAD_DIM
DKV = A_KV_HEADS * HEAD_DIM
DKV_DUP = A_KV_HEADS * LANE
AB_IN = DQ + 2 * DKV + 2 * CONV_CH
MLA_PAD = LANE
C_IN_PAD = Q_LORA + KV_LORA + LANE
MLA_KEYS = SEQ + CTX_LEN

TOK_TILE = 256
LAT_TILES = N_LAT // TOK_TILE
TILES_PER_SEQ = SEQ // TOK_TILE
OUT_TILE = 512
MOE_ROWS = 256
CONV_HALO = 16
VMEM_LIMIT = 56 * 1024 * 1024

F32 = jnp.float32
BF16 = jnp.bfloat16


def _params(*sem):
    return pltpu.CompilerParams(dimension_semantics=sem, vmem_limit_bytes=VMEM_LIMIT)


def _seg_of_tile(i, tile):
    return jnp.minimum(i * tile // SEQ, BATCH)


def _rope_block_of_tile(i):
    return jnp.where(i < LAT_TILES, i % TILES_PER_SEQ, TILES_PER_SEQ)


def _mod_spec(chunk, tile):
    return pl.BlockSpec((None, None, 1, D_MODEL), lambda i: (_seg_of_tile(i, tile), chunk, 0, 0))


def _full(shape):
    return pl.BlockSpec(shape, lambda *_: (0,) * len(shape))


def _rms(x, n=None):
    n = x.shape[-1] if n is None else n
    return x * lax.rsqrt(jnp.sum(x * x, axis=-1, keepdims=True) * (1.0 / n) + EPS)


def _modulated_norm(x_ref, g_ref, sh_ref, sc_ref):
    h = _rms(x_ref[...]) * g_ref[...]
    return h * (1.0 + sc_ref[...]) + sh_ref[...]


def _mod_kernel(c_ref, w_ref, b_ref, o_ref):
    c = c_ref[...]
    s = c * jax.nn.sigmoid(c)
    o_ref[...] = jnp.dot(s, w_ref[...], precision=lax.Precision.HIGHEST,
                         preferred_element_type=F32) + b_ref[...]


def _modulation(cvec, mod_w, mod_b):
    tn = 1536
    return pl.pallas_call(
        _mod_kernel,
        grid=(DEPTH, 6 * D_MODEL // tn),
        in_specs=[
            pl.BlockSpec((SEG_PAD, D_MODEL), lambda l, j: (0, 0)),
            pl.BlockSpec((None, D_MODEL, tn), lambda l, j: (l, 0, j)),
            pl.BlockSpec((None, 1, tn), lambda l, j: (l, 0, j)),
        ],
        out_specs=pl.BlockSpec((None, SEG_PAD, tn), lambda l, j: (l, 0, j)),
        out_shape=jax.ShapeDtypeStruct((DEPTH, SEG_PAD, 6 * D_MODEL), F32),
        compiler_params=_params("arbitrary", "arbitrary"),
        name="modulation",
    )(cvec, mod_w, mod_b.reshape(DEPTH, 1, 6 * D_MODEL))


def _rope_lanes(t, cos, sin, half0, shift):
    partner = jnp.where(half0, pltpu.roll(t, LANE - shift, 1), pltpu.roll(t, shift, 1))
    return t * cos + partner * sin


def _rope_lane_tables(rot_dims, lane_start, repeat):
    f32 = np.float32
    rows = SEQ // GRID_W
    row = np.repeat(np.arange(rows), GRID_W).astype(f32)
    col = np.tile(np.arange(GRID_W), rows).astype(f32)
    axis_dims = rot_dims // 2
    n = axis_dims // 2
    inv = (f32(ROPE_THETA) ** (-(np.arange(n, dtype=f32) * f32(2.0)) / f32(axis_dims))).astype(f32)
    ang = np.concatenate([row[:, None] * inv, col[:, None] * inv], axis=-1)
    d = np.arange(rot_dims)
    src = (d // (2 * n)) * n + d % n
    sign = np.where((d % (2 * n)) // n == 0, -1.0, 1.0).astype(f32)
    cos_t = np.ones((SEQ + TOK_TILE, LANE), f32)
    sin_t = np.zeros((SEQ + TOK_TILE, LANE), f32)
    for r in range(repeat):
        lo = lane_start + r * rot_dims
        cos_t[:SEQ, lo:lo + rot_dims] = np.cos(ang)[:, src]
        sin_t[:SEQ, lo:lo + rot_dims] = np.sin(ang)[:, src] * sign
    return jnp.asarray(cos_t), jnp.asarray(sin_t)


def _even_inproj_kernel(x_ref, g_ref, sh_ref, sc_ref, w_ref, qg_ref, kg_ref, cos_ref, sin_ref,
                        q_ref, k_ref, v_ref, z_ref):
    h = _modulated_norm(x_ref, g_ref, sh_ref, sc_ref)
    p = jnp.dot(h.astype(BF16), w_ref[...], preferred_element_type=F32)
    lane = lax.broadcasted_iota(jnp.int32, (TOK_TILE, LANE), 1)
    lo = lane < HEAD_DIM
    half0 = (lane & (HEAD_DIM // 2 - 1)) < HEAD_DIM // 4
    cos = cos_ref[...]
    sin = sin_ref[...]

    def norm_rope(t, gvec):
        ss = t * t
        s_lo = jnp.sum(jnp.where(lo, ss, 0.0), axis=-1, keepdims=True)
        s_hi = jnp.sum(jnp.where(lo, 0.0, ss), axis=-1, keepdims=True)
        ms = jnp.where(lo, s_lo, s_hi) * (1.0 / HEAD_DIM)
        tn = t * lax.rsqrt(ms + EPS) * gvec
        return _rope_lanes(tn, cos, sin, half0, HEAD_DIM // 4)

    scale = HEAD_DIM ** -0.5
    for j in range(DQ // LANE):
        q_ref[:, j * LANE:(j + 1) * LANE] = (norm_rope(p[:, j * LANE:(j + 1) * LANE], qg_ref[...]) * scale).astype(BF16)

    def twice(t, ref):
        sw = pltpu.roll(t, HEAD_DIM, 1)
        ref[:, :LANE] = jnp.where(lo, t, sw).astype(BF16)
        ref[:, LANE:] = jnp.where(lo, sw, t).astype(BF16)

    twice(norm_rope(p[:, DQ:DQ + DKV], kg_ref[...]), k_ref)
    twice(p[:, DQ + DKV:DQ + 2 * DKV], v_ref)
    u0 = DQ + 2 * DKV
    z_ref[...] = (p[:, u0:u0 + CONV_CH] * jax.nn.sigmoid(p[:, u0 + CONV_CH:])).astype(BF16)


def _even_inproj(xt, g, mods4, w_bf, qg2, kg2, cos_t, sin_t):
    row = lambda n: pl.BlockSpec((TOK_TILE, n), lambda i: (i, 0))
    tab = pl.BlockSpec((TOK_TILE, LANE), lambda i: (_rope_block_of_tile(i), 0))
    return pl.pallas_call(
        _even_inproj_kernel,
        grid=(N_TOK // TOK_TILE,),
        in_specs=[row(D_MODEL), _full((1, D_MODEL)), _mod_spec(0, TOK_TILE), _mod_spec(1, TOK_TILE),
                  _full((D_MODEL, AB_IN)), _full((1, LANE)), _full((1, LANE)), tab, tab],
        out_specs=[row(DQ), row(DKV_DUP), row(DKV_DUP), row(CONV_CH)],
        out_shape=[jax.ShapeDtypeStruct((N_TOK, DQ), BF16),
                   jax.ShapeDtypeStruct((N_TOK, DKV_DUP), BF16),
                   jax.ShapeDtypeStruct((N_TOK, DKV_DUP), BF16),
                   jax.ShapeDtypeStruct((N_TOK, CONV_CH), BF16)],
        compiler_params=_params("parallel"),
        name="even_inproj",
    )(xt, g.reshape(1, D_MODEL), mods4, mods4, w_bf, qg2, kg2, cos_t, sin_t)


def _softmax_pv(s, sk, v):
    m = jnp.maximum(jnp.max(s, axis=-1, keepdims=True), sk)
    p = jnp.exp(s - m)
    l = jnp.sum(p, axis=-1, keepdims=True) + jnp.exp(sk - m)
    o = jnp.dot(p.astype(BF16), v, preferred_element_type=F32)
    return o / l


def _nt_dot(a, b):
    return lax.dot_general(a, b, (((1,), (1,)), ((), ())), preferred_element_type=F32)


N_WIN = SEQ // WINDOW
CTX_QBLOCKS = CTX_LEN // WINDOW


def _win_attn_kernel(q_ref, kp_ref, kc_ref, kn_ref, vp_ref, vc_ref, vn_ref, kx_ref, vx_ref,
                     sink_ref, o_ref):
    n = pl.program_id(1)
    w = WINDOW
    k_all = jnp.concatenate([kp_ref[...], kc_ref[...], kn_ref[...], kx_ref[...]], axis=0)
    v_all = jnp.concatenate([vp_ref[...], vc_ref[...], vn_ref[...], vx_ref[...]], axis=0)
    nk = 3 * w + CTX_LEN
    group = A_Q_HEADS // A_KV_HEADS
    row = lax.broadcasted_iota(jnp.int32, (group * w, nk), 0)
    qi = row & (w - 1)
    ki = lax.broadcasted_iota(jnp.int32, (group * w, nk), 1)
    rel = ki - w - qi
    kpos = (n - 1) * w + ki
    kpos_end = jnp.where(n < N_WIN, SEQ, -1)
    valid = (ki >= 3 * w) | ((jnp.abs(rel) <= w) & (kpos >= 0) & (kpos < kpos_end))
    lo = lax.broadcasted_iota(jnp.int32, (w, LANE), 1) < HEAD_DIM
    head_of_row = lax.broadcasted_iota(jnp.int32, (group * w, 1), 0) // w
    tiles_per_kv = group * HEAD_DIM // LANE
    for g in range(A_KV_HEADS):
        kk = k_all[:, g * LANE:(g + 1) * LANE]
        vv = v_all[:, g * LANE:(g + 1) * LANE]
        parts = []
        for j in range(g * tiles_per_kv, (g + 1) * tiles_per_kv):
            qt = q_ref[:, j * LANE:(j + 1) * LANE]
            zero = jnp.zeros_like(qt)
            parts += [jnp.where(lo, qt, zero), jnp.where(lo, zero, qt)]
        qs = jnp.concatenate(parts, axis=0)
        sk = sink_ref[g * group:g * group + 1, 0:1]
        for hh in range(1, group):
            sk = jnp.where(head_of_row == hh, sink_ref[g * group + hh:g * group + hh + 1, 0:1], sk)
        s = jnp.where(valid, _nt_dot(qs, kk), NEG_INF)
        o = _softmax_pv(s, sk, vv)
        for t in range(tiles_per_kv):
            j = g * tiles_per_kv + t
            o_ref[:, j * LANE:(j + 1) * LANE] = jnp.where(
                lo, o[2 * t * w:(2 * t + 1) * w], o[(2 * t + 1) * w:(2 * t + 2) * w]).astype(o_ref.dtype)


def _win_attn(q, k, v, sink_b):
    def qblk(b, n):
        return (jnp.where(n < N_WIN, b * N_WIN + n, BATCH * N_WIN + b * CTX_QBLOCKS + (n - N_WIN)), 0)

    def local(off):
        return lambda b, n: (b * N_WIN + jnp.clip(n + off, 0, N_WIN - 1), 0)

    kv_spec = lambda f: pl.BlockSpec((WINDOW, DKV_DUP), f)
    ctx_spec = pl.BlockSpec((CTX_LEN, DKV_DUP), lambda b, n: (LAT_TILES + b, 0))
    return pl.pallas_call(
        _win_attn_kernel,
        grid=(BATCH, N_WIN + CTX_QBLOCKS),
        in_specs=[pl.BlockSpec((WINDOW, DQ), qblk),
                  kv_spec(local(-1)), kv_spec(local(0)), kv_spec(local(1)),
                  kv_spec(local(-1)), kv_spec(local(0)), kv_spec(local(1)),
                  ctx_spec, ctx_spec, _full((A_Q_HEADS, LANE))],
        out_specs=pl.BlockSpec((WINDOW, DQ), qblk),
        out_shape=jax.ShapeDtypeStruct((N_TOK, DQ), BF16),
        compiler_params=_params("parallel", "parallel"),
        name="win_attn",
    )(q, k, k, k, v, v, v, k, v, sink_b)


F32_SUBLANES = 8


def _conv_kernel(zp_ref, zc_ref, zn_ref, w_ref, b_ref, g_ref, bb_ref, o_ref, pad_ref, shift_ref):
    i = pl.program_id(0)
    tl = TOK_TILE
    h = CONV_HALO
    pos = i % TILES_PER_SEQ
    first = (i >= LAT_TILES) | (pos == 0)
    last = (i >= LAT_TILES) | (pos == TILES_PER_SEQ - 1)
    pad_ref[0:h, :] = jnp.where(first, 0.0, zp_ref[...].astype(F32))
    pad_ref[h:h + tl, :] = zc_ref[...].astype(F32)
    pad_ref[h + tl:2 * h + tl, :] = jnp.where(last, 0.0, zn_ref[...].astype(F32))
    span = tl + 2 * h - F32_SUBLANES
    for r in range(1, F32_SUBLANES):
        shift_ref[r - 1] = pad_ref[r:r + span, :]

    def tap(j):
        start = h - CONV_WIDTH // 2 + j
        r = start % F32_SUBLANES
        src = pad_ref if r == 0 else shift_ref.at[r - 1]
        return src[start - r:start - r + tl, :] * w_ref[j:j + 1, :]

    acc = tap(0)
    for j in range(1, CONV_WIDTH):
        acc = acc + tap(j)
    z = acc + b_ref[...]
    mu = jnp.mean(z, axis=-1, keepdims=True)
    zc = z - mu
    y = zc * lax.rsqrt(jnp.mean(zc * zc, axis=-1, keepdims=True) + EPS)
    y = y * g_ref[...] + bb_ref[...]
    o_ref[...] = (y * jax.nn.sigmoid(y)).astype(o_ref.dtype)


def _conv_module(z, dw_w, dw_b, ln_g, ln_b):
    r = TOK_TILE // CONV_HALO
    n_halo = N_TOK // CONV_HALO
    halo = lambda f: pl.BlockSpec((CONV_HALO, CONV_CH), f)
    vec = _full((1, CONV_CH))
    return pl.pallas_call(
        _conv_kernel,
        grid=(N_TOK // TOK_TILE,),
        in_specs=[halo(lambda i: (jnp.maximum(i * r - 1, 0), 0)),
                  pl.BlockSpec((TOK_TILE, CONV_CH), lambda i: (i, 0)),
                  halo(lambda i: (jnp.minimum((i + 1) * r, n_halo - 1), 0)),
                  _full((CONV_WIDTH, CONV_CH)), vec, vec, vec],
        out_specs=pl.BlockSpec((TOK_TILE, CONV_CH), lambda i: (i, 0)),
        out_shape=jax.ShapeDtypeStruct((N_TOK, CONV_CH), BF16),
        scratch_shapes=[pltpu.VMEM((TOK_TILE + 2 * CONV_HALO, CONV_CH), F32),
                        pltpu.VMEM((F32_SUBLANES - 1, TOK_TILE + 2 * CONV_HALO - F32_SUBLANES, CONV_CH), F32)],
        compiler_params=_params("parallel"),
        name="conv_module",
    )(z, z, z, dw_w, dw_b.reshape(1, -1), ln_g.reshape(1, -1), ln_b.reshape(1, -1))


def _mla_proj_kernel(x_ref, g_ref, sh_ref, sc_ref, win_ref, qag_ref, kvag_ref, krg_ref, wq_ref, wkv_ref,
                     qg_ref, kg_ref, cos_ref, sin_ref, q_ref, k_ref, v_ref):
    h = _modulated_norm(x_ref, g_ref, sh_ref, sc_ref)
    p = jnp.dot(h.astype(BF16), win_ref[...], preferred_element_type=F32)
    lane = lax.broadcasted_iota(jnp.int32, (TOK_TILE, LANE), 1)
    half0 = (lane & (MLA_ROPE // 2 - 1)) < MLA_ROPE // 4
    cos = cos_ref[...]
    sin = sin_ref[...]
    rope = lambda t: _rope_lanes(t, cos, sin, half0, MLA_ROPE // 4)

    cq = _rms(p[:, :Q_LORA]) * qag_ref[...]
    ckv = _rms(p[:, Q_LORA:Q_LORA + KV_LORA]) * kvag_ref[...]
    kr = rope(_rms(p[:, Q_LORA + KV_LORA:], MLA_ROPE) * krg_ref[...])

    q = jnp.dot(cq.astype(BF16), wq_ref[...], preferred_element_type=F32)
    scale = float(MLA_QK ** -0.5 * np.log2(np.e))
    for hd in range(MLA_HEADS):
        qh = _rms(q[:, hd * MLA_PAD:(hd + 1) * MLA_PAD], MLA_QK) * qg_ref[...]
        q_ref[:, hd * MLA_PAD:(hd + 1) * MLA_PAD] = (rope(qh) * scale).astype(BF16)

    kv = jnp.dot(ckv.astype(BF16), wkv_ref[...], preferred_element_type=F32)
    for hd in range(MLA_HEADS):
        kh = _rms(kv[:, hd * MLA_PAD:(hd + 1) * MLA_PAD], MLA_NOPE) * kg_ref[...]
        k_ref[:, hd * MLA_PAD:(hd + 1) * MLA_PAD] = (kh + kr).astype(BF16)
    v_ref[...] = kv[:, MLA_HEADS * MLA_PAD:].astype(BF16)


def _mla_proj(xt, g, mods4, w_in, qa_g, kva_g, kr_g, w_q, w_kv, q_g, k_g, cos_t, sin_t):
    tab = pl.BlockSpec((TOK_TILE, LANE), lambda i: (_rope_block_of_tile(i), 0))

    def key_blk(i):
        return (jnp.where(i < LAT_TILES, i // TILES_PER_SEQ, i - LAT_TILES), _rope_block_of_tile(i), 0)

    kv_n = MLA_HEADS * (MLA_PAD + MLA_V)
    return pl.pallas_call(
        _mla_proj_kernel,
        grid=(N_TOK // TOK_TILE,),
        in_specs=[pl.BlockSpec((TOK_TILE, D_MODEL), lambda i: (i, 0)), _full((1, D_MODEL)),
                  _mod_spec(0, TOK_TILE), _mod_spec(1, TOK_TILE),
                  _full((D_MODEL, C_IN_PAD)), _full((1, Q_LORA)), _full((1, KV_LORA)), _full((1, LANE)),
                  _full((Q_LORA, MLA_HEADS * MLA_PAD)), _full((KV_LORA, kv_n)),
                  _full((1, LANE)), _full((1, LANE)), tab, tab],
        out_specs=[pl.BlockSpec((TOK_TILE, MLA_HEADS * MLA_PAD), lambda i: (i, 0)),
                   pl.BlockSpec((None, TOK_TILE, MLA_HEADS * MLA_PAD), key_blk),
                   pl.BlockSpec((None, TOK_TILE, MLA_HEADS * MLA_V), key_blk)],
        out_shape=[jax.ShapeDtypeStruct((N_TOK, MLA_HEADS * MLA_PAD), BF16),
                   jax.ShapeDtypeStruct((BATCH, MLA_KEYS, MLA_HEADS * MLA_PAD), BF16),
                   jax.ShapeDtypeStruct((BATCH, MLA_KEYS, MLA_HEADS * MLA_V), BF16)],
        compiler_params=_params("parallel"),
        name="mla_proj",
    )(xt, g.reshape(1, D_MODEL), mods4, mods4, w_in, qa_g, kva_g, kr_g, w_q, w_kv, q_g, k_g, cos_t, sin_t)


MLA_GROUP = 8
MLA_Q_TILE = 256


def _mla_kernel(q_ref, k_ref, v_ref, *rest):
    o_ref = rest[-1]
    lane = lax.broadcasted_iota(jnp.int32, (q_ref.shape[0], 2 * MLA_V), 1)
    for jp in range(MLA_GROUP // 2):
        outs = []
        v2 = v_ref[:, jp * 2 * MLA_V:(jp + 1) * 2 * MLA_V]
        for j in range(2 * jp, 2 * jp + 2):
            qh = q_ref[:, j * MLA_PAD:(j + 1) * MLA_PAD]
            kh = k_ref[:, j * MLA_PAD:(j + 1) * MLA_PAD]
            s = _nt_dot(qh, kh)
            m = jnp.max(s, axis=-1, keepdims=True)
            p = jnp.exp2(s - m)
            l = jnp.sum(p, axis=-1, keepdims=True)
            outs.append(jnp.dot(p.astype(BF16), v2, preferred_element_type=F32) / l)
        o_ref[:, jp * 2 * MLA_V:(jp + 1) * 2 * MLA_V] = jnp.where(lane < MLA_V, outs[0], outs[1]).astype(o_ref.dtype)


def _mla_attn_latent(q, k, v, n_out_rows):
    nq = SEQ // MLA_Q_TILE
    qblk = lambda b, p, i: (b * nq + i, p)
    return pl.pallas_call(
        _mla_kernel,
        grid=(BATCH, MLA_HEADS // MLA_GROUP, nq),
        in_specs=[pl.BlockSpec((MLA_Q_TILE, MLA_GROUP * MLA_PAD), qblk),
                  pl.BlockSpec((None, MLA_KEYS, MLA_GROUP * MLA_PAD), lambda b, p, i: (b, 0, p)),
                  pl.BlockSpec((None, MLA_KEYS, MLA_GROUP * MLA_V), lambda b, p, i: (b, 0, p))],
        out_specs=pl.BlockSpec((MLA_Q_TILE, MLA_GROUP * MLA_V), qblk),
        out_shape=jax.ShapeDtypeStruct((n_out_rows, MLA_HEADS * MLA_V), BF16),
        compiler_params=_params("parallel", "parallel", "parallel"),
        name="mla_attn",
    )(q, k, v)


def _mla_attn_ctx(q, k, v, o):
    qblk = lambda b, p: (LAT_TILES + b, p)
    tail = lambda b, p: (b, TILES_PER_SEQ, p)
    return pl.pallas_call(
        _mla_kernel,
        grid=(BATCH, MLA_HEADS // MLA_GROUP),
        in_specs=[pl.BlockSpec((TOK_TILE, MLA_GROUP * MLA_PAD), qblk),
                  pl.BlockSpec((None, CTX_LEN, MLA_GROUP * MLA_PAD), tail),
                  pl.BlockSpec((None, CTX_LEN, MLA_GROUP * MLA_V), tail),
                  pl.BlockSpec(memory_space=pl.ANY)],
        out_specs=pl.BlockSpec((TOK_TILE, MLA_GROUP * MLA_V), qblk),
        out_shape=jax.ShapeDtypeStruct((N_TOK, MLA_HEADS * MLA_V), BF16),
        input_output_aliases={3: 0},
        compiler_params=_params("parallel", "parallel"),
        name="mla_attn_ctx",
    )(q, k, v, o)


def _split_bf16(a):
    hi = a.astype(BF16)
    return hi, (a - hi.astype(F32)).astype(BF16)


def _outproj_kernel(n_in, *refs):
    a_refs = refs[:n_in]
    w_refs = refs[n_in:2 * n_in]
    x_ref, g1_ref, gn_ref, sh_ref, sc_ref, rw_ref, rb_ref, xo_ref, f_ref, lg_ref = refs[2 * n_in:]
    y = jnp.dot(a_refs[0][...], w_refs[0][...], preferred_element_type=F32)
    for a_ref, w_ref in zip(a_refs[1:], w_refs[1:]):
        y = y + jnp.dot(a_ref[...], w_ref[...], preferred_element_type=F32)
    xo_ref[...] = x_ref[...] + g1_ref[...] * y
    f = _modulated_norm(xo_ref, gn_ref, sh_ref, sc_ref)
    _store_token_tiles(f_ref, f)
    f_hi, f_lo = _split_bf16(f)
    w_hi, w_lo = _split_bf16(rw_ref[...])
    mm = lambda a, b: jnp.dot(a, b, preferred_element_type=F32)
    lg_ref[...] = mm(f_hi, w_hi) + (mm(f_lo, w_hi) + mm(f_hi, w_lo)) + rb_ref[...]


def _outproj(a_list, w_list, x, n_rows, mods4, gn, rw_pad, rb_pad):
    n_in = len(a_list)
    row = lambda n: pl.BlockSpec((OUT_TILE, n), lambda i: (i, 0))
    in_specs = [row(a.shape[1]) for a in a_list] + [_full(w.shape) for w in w_list]
    in_specs += [row(D_MODEL), _mod_spec(2, OUT_TILE), _full((1, D_MODEL)),
                 _mod_spec(3, OUT_TILE), _mod_spec(4, OUT_TILE), _full((D_MODEL, LANE)), _full((1, LANE))]
    return pl.pallas_call(
        functools.partial(_outproj_kernel, n_in),
        grid=(n_rows // OUT_TILE,),
        in_specs=in_specs,
        out_specs=[row(D_MODEL), pl.BlockSpec((OUT_TILE * ROW_SUB, LANE), lambda i: (i, 0)), row(LANE)],
        out_shape=[jax.ShapeDtypeStruct((n_rows, D_MODEL), F32),
                   jax.ShapeDtypeStruct((n_rows * ROW_SUB, LANE), F32),
                   jax.ShapeDtypeStruct((n_rows, LANE), F32)],
        compiler_params=_params("parallel"),
        name="outproj",
    )(*a_list, *w_list, x, mods4, gn.reshape(1, D_MODEL), mods4, mods4, rw_pad, rb_pad)


IDX_LANE = 0
GATE_LANE = TOP_K
POS_LANE = 2 * TOP_K
MASKED = -3.0e38
ROUTE_TILE = 1024


def _route_kernel(lg_ref, info_ref, cnt_ref, carry_ref):
    i = pl.program_id(0)

    @pl.when(i == 0)
    def _():
        carry_ref[...] = jnp.zeros_like(carry_ref)

    lane = lax.broadcasted_iota(jnp.int32, (ROUTE_TILE, LANE), 1).astype(F32)
    x = jnp.where(lane < N_EXPERTS, lg_ref[...], MASKED)
    vals, idxs = [], []
    for _ in range(TOP_K):
        m = jnp.max(x, axis=-1, keepdims=True)
        idx = jnp.min(jnp.where(x == m, lane, float(LANE)), axis=-1, keepdims=True)
        vals.append(m)
        idxs.append(idx)
        x = jnp.where(lane == idx, MASKED, x)
    es = [jnp.exp(v - vals[0]) for v in vals]
    denom = es[0] + es[1] + es[2] + es[3]
    onehot = jnp.zeros((ROUTE_TILE, LANE), F32)
    for idx in idxs:
        onehot = onehot + jnp.where(lane == idx, 1.0, 0.0)
    r = lax.broadcasted_iota(jnp.int32, (ROUTE_TILE, ROUTE_TILE), 0)
    c = lax.broadcasted_iota(jnp.int32, (ROUTE_TILE, ROUTE_TILE), 1)
    tri = jnp.where(r > c, 1.0, 0.0).astype(BF16)
    before = jnp.dot(tri, onehot.astype(BF16), preferred_element_type=F32) + carry_ref[...]
    info = jnp.zeros((ROUTE_TILE, LANE), F32)
    for k in range(TOP_K):
        pos =jnp.sum(jnp.where(lane == idxs[k], before, 0.0), axis=-1, keepdims=True)
        info = jnp.where(lane == IDX_LANE + k, idxs[k], info)
        info = jnp.where(lane == GATE_LANE + k, es[k] / denom, info)
        info = jnp.where(lane == POS_LANE + k, pos, info)
    info_ref[...] = info
    carry_ref[...] = carry_ref[...] + jnp.sum(onehot, axis=0, keepdims=True)
    cnt_ref[...] = carry_ref[...]


def _route(logits, n_rows):
    return pl.pallas_call(
        _route_kernel,
        grid=(n_rows // ROUTE_TILE,),
        in_specs=[pl.BlockSpec((ROUTE_TILE, LANE), lambda i: (i, 0))],
        out_specs=[pl.BlockSpec((ROUTE_TILE, LANE), lambda i: (i, 0)), _full((1, LANE))],
        out_shape=[jax.ShapeDtypeStruct((n_rows, LANE), F32), jax.ShapeDtypeStruct((1, LANE), F32)],
        scratch_shapes=[pltpu.VMEM((1, LANE), F32)],
        compiler_params=_params("arbitrary"),
        name="route",
    )(logits)


ROW_SUB = D_MODEL // LANE


def _store_token_tiles(ref, val):
    n = val.shape[0]
    for s in range(ROW_SUB):
        ref[pl.ds(s, n, stride=ROW_SUB), :] = val[:, s * LANE:(s + 1) * LANE]


def _load_token_tiles(ref, n):
    return jnp.concatenate([ref[pl.ds(s, n, stride=ROW_SUB), :] for s in range(ROW_SUB)], axis=1)


def _tile_copy(src_ref, src_row, dst_ref, dst_row, sem):
    return pltpu.make_async_copy(src_ref.at[pl.ds(pl.multiple_of(src_row, ROW_SUB), ROW_SUB)],
                                 dst_ref.at[pl.ds(pl.multiple_of(dst_row, ROW_SUB), ROW_SUB)], sem)


def _dispatch_kernel(dest_ref, f_ref, xb_ref, sem):
    def issue(r, carry):
        for k in range(TOP_K):
            _tile_copy(f_ref, r * ROW_SUB, xb_ref, dest_ref[0, r * TOP_K + k], sem).start(priority=k % 2)
        return carry

    lax.fori_loop(0, TOK_TILE, issue, 0, unroll=8)
    for _ in range(TOP_K):
        pltpu.make_async_copy(f_ref, xb_ref.at[pl.ds(0, TOK_TILE * ROW_SUB)], sem).wait()


def _dispatch(dest3, f, n_rows_out):
    n_rows = dest3.shape[0] * TOK_TILE
    return pl.pallas_call(
        _dispatch_kernel,
        grid=(n_rows // TOK_TILE,),
        in_specs=[pl.BlockSpec((None, 1, TOK_TILE * TOP_K), lambda i: (i, 0, 0), memory_space=pltpu.SMEM),
                  pl.BlockSpec((TOK_TILE * ROW_SUB, LANE), lambda i: (i, 0))],
        out_specs=pl.BlockSpec(memory_space=pl.ANY),
        out_shape=jax.ShapeDtypeStruct((n_rows_out * ROW_SUB, LANE), F32),
        scratch_shapes=[pltpu.SemaphoreType.DMA(())],
        compiler_params=_params("arbitrary"),
        name="moe_dispatch",
    )(dest3, f)


def _moe_kernel(layer, be_ref, nv_ref, first_ref, nxt_ref, x_ref, wgu_hbm, bgu_ref, wd_hbm, bd_ref, o_ref,
                wgu_f32, wd_f32, wgu_s, wd_s, sems):
    i = pl.program_id(0)

    def fetch(e):
        return (pltpu.make_async_copy(wgu_hbm.at[layer, e], wgu_f32, sems.at[0]),
                pltpu.make_async_copy(wd_hbm.at[layer, e], wd_f32, sems.at[1]))

    @pl.when(i == 0)
    def _():
        for cp in fetch(be_ref[0]):
            cp.start()

    @pl.when(first_ref[i] == 1)
    def _():
        for cp in fetch(be_ref[i]):
            cp.wait()
        wgu_s[...] = wgu_f32[...].astype(BF16)
        wd_s[...] = wd_f32[...].astype(BF16)

        @pl.when(nxt_ref[i] >= 0)
        def _():
            for cp in fetch(nxt_ref[i]):
                cp.start()

    @pl.when(nv_ref[i] == 0)
    def _():
        o_ref[...] = jnp.zeros_like(o_ref)

    @pl.when(nv_ref[i] > 0)
    def _():
        row = lax.broadcasted_iota(jnp.int32, (MOE_ROWS, D_MODEL), 0)
        x = _load_token_tiles(x_ref, MOE_ROWS)
        x = jnp.where(row < nv_ref[i], x, 0.0).astype(BF16)
        gu = jnp.dot(x, wgu_s[...], preferred_element_type=F32) + bgu_ref[...]
        gate = jnp.minimum(gu[:, :D_EXPERT], SWIGLU_LIMIT)
        up = jnp.clip(gu[:, D_EXPERT:], -SWIGLU_LIMIT, SWIGLU_LIMIT)
        act = (up + 1.0) * (gate * jax.nn.sigmoid(SWIGLU_ALPHA * gate))
        _store_token_tiles(o_ref, jnp.dot(act.astype(BF16), wd_s[...], preferred_element_type=F32) + bd_ref[...])


def _moe_experts(layer, block_expert, rows_valid, first, nxt, xb, w_gu, b_gu, w_down, b_down):
    n_rows = xb.shape[0] // ROW_SUB
    bsel = lambda i, be, nv, fi, nx: (layer, be[i], 0, 0)
    rows = lambda i, be, nv, fi, nx: (i, 0)
    grid_spec = pltpu.PrefetchScalarGridSpec(
        num_scalar_prefetch=4,
        grid=(n_rows // MOE_ROWS,),
        in_specs=[
            pl.BlockSpec((MOE_ROWS * ROW_SUB, LANE), rows),
            pl.BlockSpec(memory_space=pl.ANY),
            pl.BlockSpec((None, None, 1, 2 * D_EXPERT), bsel),
            pl.BlockSpec(memory_space=pl.ANY),
            pl.BlockSpec((None, None, 1, D_MODEL), bsel),
        ],
        out_specs=pl.BlockSpec((MOE_ROWS * ROW_SUB, LANE), rows),
        scratch_shapes=[pltpu.VMEM((D_MODEL, 2 * D_EXPERT), F32),
                        pltpu.VMEM((D_EXPERT, D_MODEL), F32),
                        pltpu.VMEM((D_MODEL, 2 * D_EXPERT), BF16),
                        pltpu.VMEM((D_EXPERT, D_MODEL), BF16),
                        pltpu.SemaphoreType.DMA((2,))],
    )
    return pl.pallas_call(
        functools.partial(_moe_kernel, layer),
        grid_spec=grid_spec,
        out_shape=jax.ShapeDtypeStruct((n_rows * ROW_SUB, LANE), F32),
        compiler_params=_params("arbitrary"),
        name="moe_experts",
    )(block_expert, rows_valid, first, nxt, xb, w_gu, b_gu.reshape(DEPTH, N_EXPERTS, 1, -1), w_down,
      b_down.reshape(DEPTH, N_EXPERTS, 1, -1))


def _combine_kernel(dest_ref, destn_ref, x_ref, g2_ref, info_ref, yb_ref, o_ref, buf, gate_ref, sems):
    i = pl.program_id(0)
    slot = i % 2

    def issue(d_ref, sl):
        def body(r, carry):
            for k in range(TOP_K):
                _tile_copy(yb_ref, d_ref[0, r * TOP_K + k], buf.at[sl, k], r * ROW_SUB,
                           sems.at[sl]).start(priority=k % 2)
            return carry

        lax.fori_loop(0, TOK_TILE, body, 0, unroll=8)

    @pl.when(i == 0)
    def _():
        issue(dest_ref, 0)

    @pl.when(i + 1 < pl.num_programs(0))
    def _():
        issue(destn_ref, 1 - slot)

    for k in range(TOP_K):
        pltpu.make_async_copy(yb_ref.at[pl.ds(0, TOK_TILE * ROW_SUB)], buf.at[slot, k], sems.at[slot]).wait()
    info = info_ref[...]
    for k in range(TOP_K):
        gate_ref[k] = jnp.broadcast_to(info[:, GATE_LANE + k:GATE_LANE + k + 1], (TOK_TILE, LANE))
    for s in range(ROW_SUB):
        y = None
        for k in range(TOP_K):
            term = gate_ref[k] * buf[slot, k, pl.ds(s, TOK_TILE, stride=ROW_SUB), :]
            y = term if y is None else y + term
        sl = slice(s * LANE, (s + 1) * LANE)
        o_ref[:, sl] = x_ref[:, sl] + g2_ref[:, sl] * y


def _combine(dest3, x, mods4, info, yb):
    tiles = dest3.shape[0]
    n_rows = tiles * TOK_TILE
    row = lambda n: pl.BlockSpec((TOK_TILE, n), lambda i: (i, 0))
    dest_spec = lambda f: pl.BlockSpec((None, 1, TOK_TILE * TOP_K), f, memory_space=pltpu.SMEM)
    return pl.pallas_call(
        _combine_kernel,
        grid=(tiles,),
        in_specs=[dest_spec(lambda i: (i, 0, 0)), dest_spec(lambda i: (jnp.minimum(i + 1, tiles - 1), 0, 0)),
                  row(D_MODEL), _mod_spec(5, TOK_TILE), row(LANE), pl.BlockSpec(memory_space=pl.ANY)],
        out_specs=row(D_MODEL),
        out_shape=jax.ShapeDtypeStruct((n_rows, D_MODEL), F32),
        scratch_shapes=[pltpu.VMEM((2, TOP_K, TOK_TILE * ROW_SUB, LANE), F32),
                        pltpu.VMEM((TOP_K, TOK_TILE, LANE), F32),
                        pltpu.SemaphoreType.DMA((2,))],
        compiler_params=_params("arbitrary"),
        name="moe_combine",
    )(dest3, dest3, x, mods4, info, yb)


def _moe_layer(layer, x_new, f, logits, n_rows, mods4, w_gu, b_gu, w_down, b_down):
    info, cnt = _route(logits, n_rows)
    idx = info[:, IDX_LANE:IDX_LANE + TOP_K].astype(jnp.int32)
    pos = info[:, POS_LANE:POS_LANE + TOP_K].astype(jnp.int32)
    counts = cnt[0, :N_EXPERTS].astype(jnp.int32)
    padded = (counts + MOE_ROWS - 1) // MOE_ROWS * MOE_ROWS
    pad_ends = jnp.cumsum(padded)
    pad_starts = pad_ends - padded
    experts = jnp.arange(N_EXPERTS, dtype=jnp.int32)
    dest = pos + jnp.sum(jnp.where(idx[:, :, None] == experts, pad_starts, 0), axis=-1)
    dest3 = (dest * ROW_SUB).reshape(n_rows // TOK_TILE, 1, TOK_TILE * TOP_K)
    n_blocks = n_rows * TOP_K // MOE_ROWS + N_EXPERTS
    block_start = jnp.arange(n_blocks, dtype=jnp.int32) * MOE_ROWS
    used = counts > 0
    last_used = jnp.max(jnp.where(used, experts, 0))
    block_expert = jnp.minimum(jnp.sum((block_start[:, None] >= pad_ends[None, :]).astype(jnp.int32), axis=1),
                               last_used)
    group_end = (pad_starts + counts)[block_expert]
    rows_valid = jnp.clip(group_end - block_start, 0, MOE_ROWS).astype(jnp.int32)
    first = jnp.concatenate([jnp.ones((1,), jnp.int32),
                             (block_expert[1:] != block_expert[:-1]).astype(jnp.int32)])
    later = jnp.where(used[None, :] & (experts[None, :] > experts[:, None]), experts[None, :], N_EXPERTS)
    next_used = jnp.min(later, axis=1)
    nxt = jnp.where(next_used < N_EXPERTS, next_used, -1)[block_expert].astype(jnp.int32)
    xb = _dispatch(dest3, f, n_blocks * MOE_ROWS)
    yb = _moe_experts(layer, block_expert, rows_valid, first, nxt, xb, w_gu, b_gu, w_down, b_down)
    return _combine(dest3, x_new, mods4, info, yb)


def _pad_lanes(v, start=0):
    return jnp.zeros((1, LANE), F32).at[0, start:start + v.shape[0]].set(v)


def kernel(x, c, ctx, c_ctx, mod_w, mod_b, norm_mix_g, norm_ffn_g, ab_w_in, ab_w_out, a_q_norm, a_k_norm, a_sink, b_dw_w, b_dw_b, b_ln_g, b_ln_b, c_w_in, c_q_a_norm, c_kv_a_norm, c_w_q_b, c_w_kv_b, c_q_norm, c_k_norm, c_kr_norm, c_w_out, router_w, router_b, exp_w_gu, exp_b_gu, exp_w_down, exp_b_down):
    cos_a, sin_a = _rope_lane_tables(HEAD_DIM, 0, LANE // HEAD_DIM)
    cos_c, sin_c = _rope_lane_tables(MLA_ROPE, MLA_NOPE, 1)
    cvec = jnp.concatenate([c, c_ctx[None, :], jnp.zeros((SEG_PAD - N_SEG, D_MODEL), F32)], axis=0)
    mods = _modulation(cvec, mod_w, mod_b)
    xt = jnp.concatenate([x.reshape(N_LAT, D_MODEL), ctx.reshape(N_CTX, D_MODEL)], axis=0)
    for layer in range(DEPTH):
        keep_ctx = layer < DEPTH - 1
        n_rows = N_TOK if keep_ctx else N_LAT
        mods4 = mods[layer].reshape(SEG_PAD, 6, 1, D_MODEL)
        i = layer // 2
        if layer % 2 == 0:
            qg2 = jnp.tile(a_q_norm[i], LANE // HEAD_DIM).reshape(1, LANE)
            kg2 = jnp.tile(a_k_norm[i], LANE // HEAD_DIM).reshape(1, LANE)
            q, k, v, z = _even_inproj(xt, norm_mix_g[layer], mods4, ab_w_in[i].astype(BF16), qg2, kg2, cos_a, sin_a)
            sink_b = jnp.broadcast_to(a_sink[i].astype(F32)[:, None], (A_Q_HEADS, LANE))
            a = _win_attn(q, k, v, sink_b)
            b = _conv_module(z, b_dw_w[i], b_dw_b[i], b_ln_g[i], b_ln_b[i])
            w_out = ab_w_out[i].astype(BF16)
            a_list, w_list = [a, b], [w_out[:DQ], w_out[DQ:]]
        else:
            w_in = c_w_in[i]
            zeros = lambda n: jnp.zeros((D_MODEL, n), F32)
            w_in_p = jnp.concatenate([w_in[:, :Q_LORA + KV_LORA], zeros(MLA_NOPE), w_in[:, Q_LORA + KV_LORA:],
                                      zeros(LANE - MLA_QK)], axis=1).astype(BF16)
            w_q = jnp.pad(c_w_q_b[i].reshape(Q_LORA, MLA_HEADS, MLA_QK),
                          ((0, 0), (0, 0), (0, MLA_PAD - MLA_QK))).reshape(Q_LORA, MLA_HEADS * MLA_PAD)
            w_kv = c_w_kv_b[i].reshape(KV_LORA, MLA_HEADS, MLA_NOPE + MLA_V)
            w_k = jnp.pad(w_kv[:, :, :MLA_NOPE], ((0, 0), (0, 0), (0, MLA_PAD - MLA_NOPE)))
            w_kv_p = jnp.concatenate([w_k.reshape(KV_LORA, -1), w_kv[:, :, MLA_NOPE:].reshape(KV_LORA, -1)], axis=1)
            q, k, v = _mla_proj(xt, norm_mix_g[layer], mods4, w_in_p, c_q_a_norm[i].reshape(1, -1),
                                c_kv_a_norm[i].reshape(1, -1), _pad_lanes(c_kr_norm[i], MLA_NOPE),
                                w_q.astype(BF16), w_kv_p.astype(BF16), _pad_lanes(c_q_norm[i]),
                                _pad_lanes(c_k_norm[i]), cos_c, sin_c)
            o = _mla_attn_latent(q, k, v, n_rows)
            if keep_ctx:
                o = _mla_attn_ctx(q, k, v, o)
            a_list, w_list = [o], [c_w_out[i].astype(BF16)]
        rw_pad = jnp.pad(router_w[layer], ((0, 0), (0, LANE - N_EXPERTS)))
        rb_pad = jnp.pad(router_b[layer], (0, LANE - N_EXPERTS)).reshape(1, LANE)
        x_new, f, logits = _outproj(a_list, w_list, xt, n_rows, mods4, norm_ffn_g[layer], rw_pad, rb_pad)
        xt = _moe_layer(layer, x_new, f, logits, n_rows, mods4, exp_w_gu, exp_b_gu, exp_w_down, exp_b_down)
    return xt.reshape(BATCH, SEQ, D_MODEL)
```
